```python
import jax, jax.numpy as jnp
from jax import lax
import numpy as np

D_MODEL = 2048
BATCH = 16
SEQ = 2048
DEPTH = 4

D_MIX = D_MODEL
HG_WIDTH = D_MIX // 2
HG_HEADS = 8
HG_DK = HG_WIDTH // HG_HEADS
HG_DV = HG_WIDTH // HG_HEADS
HG_CHUNK = 64
ML_WIDTH = D_MIX - HG_WIDTH
ML_HEADS = 4
ML_DV = ML_WIDTH // ML_HEADS
ML_DQK = ML_DV // 2
ML_QK = ML_HEADS * ML_DQK
ML_CHUNK = 64
CONV_W = 4
FFN_DENSE = 5504
N_EXPERTS = 8
TOP_K = 2
FFN_EXPERT = 7168
MOE_BLOCK = 256
N_DENSE_LAYERS = (DEPTH + 1) // 2
N_MOE_LAYERS = DEPTH // 2
ALPHA = (2 * DEPTH) ** 0.25
BETA = (8 * DEPTH) ** -0.25
LN_EPS = 1e-5
RMS_EPS = 1e-6
NEG_BIG = -1e30
F_MIN = 1e-30
IN_SPLITS = (HG_WIDTH, HG_WIDTH, HG_WIDTH, HG_WIDTH, ML_QK, ML_QK, ML_WIDTH, ML_WIDTH, ML_HEADS, ML_HEADS)

kernel_name = "hymba_style_hgrn2_mlstm_moe_deepnorm"


def layer_norm(x, g, b):
    xf = x.astype(jnp.float32)
    mu = xf.mean(-1, keepdims=True)
    var = jnp.square(xf - mu).mean(-1, keepdims=True)
    return ((xf - mu) * lax.rsqrt(var + LN_EPS) * g + b).astype(x.dtype)


def head_rms_norm(x, g, n_heads):
    B, S, W = x.shape
    xh = x.reshape(B, S, n_heads, W // n_heads)
    xh = xh * lax.rsqrt(jnp.mean(jnp.square(xh), -1, keepdims=True) + RMS_EPS)
    return xh.reshape(B, S, W) * g.astype(jnp.float32)


def to_chunks(t, n_heads, chunk):
    B, S, _ = t.shape
    return t.reshape(B, S // chunk, chunk, n_heads, -1).transpose(1, 0, 3, 2, 4)


def scalars_to_chunks(t, chunk):
    B, S, H = t.shape
    return t.reshape(B, S // chunk, chunk, H).transpose(1, 0, 3, 2)


def from_chunks(t):
    N, B, H, L, d = t.shape
    return t.transpose(1, 0, 3, 2, 4).reshape(B, N * L, H * d)


def causal_depthwise_conv(x, w, b):
    C = x.shape[-1]
    y = lax.conv_general_dilated(x, w[:, None, :].astype(x.dtype), window_strides=(1,),
                                 padding=[(CONV_W - 1, 0)], dimension_numbers=("NWC", "WIO", "NWC"),
                                 feature_group_count=C)
    return y + b.astype(x.dtype)


def hgrn2_chunk_step(state, chunk):
    q, k, v, log_f = chunk
    L = q.shape[2]
    causal = jnp.tril(jnp.ones((L, L), dtype=bool))
    b = jnp.cumsum(log_f, axis=2)
    diff = b[:, :, :, None, :] - b[:, :, None, :, :]
    decay = jnp.exp(jnp.where(causal[:, :, None], diff, NEG_BIG))
    scores = jnp.einsum("bhtk,bhtsk,bhsk->bhts", q, decay, k)
    out = jnp.einsum("bhts,bhsv->bhtv", scores, v) + jnp.einsum("bhtk,bhkv->bhtv", q * jnp.exp(b), state)
    b_last = b[:, :, -1:, :]
    new_state = (jnp.exp(b_last[:, :, 0, :])[..., None] * state
                 + jnp.einsum("bhsk,bhsv->bhkv", k * jnp.exp(b_last - b), v))
    return new_state, out


def hgrn2_mixer(q_pre, f_pre, i_pre, g_pre, lower_bound, norm_g):
    B = q_pre.shape[0]
    f32 = jnp.float32
    q = jax.nn.silu(q_pre.astype(f32))
    lb = lower_bound.astype(f32)
    fp = f_pre.astype(f32)
    f = lb + (1.0 - lb) * jax.nn.sigmoid(fp)
    log_f = jnp.log(jnp.maximum(f, F_MIN))
    k = (1.0 - lb) * jax.nn.sigmoid(-fp)
    v = i_pre.astype(f32)
    s0 = jnp.zeros((B, HG_HEADS, HG_DK, HG_DV), f32)
    _, o = lax.scan(hgrn2_chunk_step, s0,
                    (to_chunks(q, HG_HEADS, HG_CHUNK), to_chunks(k, HG_HEADS, HG_CHUNK),
                     to_chunks(v, HG_HEADS, HG_CHUNK), to_chunks(log_f, HG_HEADS, HG_CHUNK)))
    o = from_chunks(o)
    return head_rms_norm(o, norm_g, HG_HEADS) * jax.nn.silu(g_pre.astype(f32))


def mlstm_chunk_step(carry, chunk):
    c_state, n_state, m_state = carry
    q, k, v, log_i, log_f = chunk
    L = q.shape[2]
    causal = jnp.tril(jnp.ones((L, L), dtype=bool))
    b = jnp.cumsum(log_f, axis=-1)
    log_intra = jnp.where(causal, b[..., :, None] - b[..., None, :] + log_i[..., None, :], NEG_BIG)
    log_inter = b + m_state[..., None]
    m_t = jnp.maximum(log_inter, log_intra.max(-1))
    w_intra = jnp.exp(log_intra - m_t[..., None])
    w_inter = jnp.exp(log_inter - m_t)
    scores = jnp.einsum("bhtk,bhsk->bhts", q, k) * w_intra
    num = jnp.einsum("bhts,bhsv->bhtv", scores, v) + w_inter[..., None] * jnp.einsum("bhtk,bhkv->bhtv", q, c_state)
    den = scores.sum(-1) + w_inter * jnp.einsum("bhtk,bhk->bht", q, n_state)
    h = num / jnp.maximum(jnp.abs(den), jnp.exp(-m_t))[..., None]
    m_new = m_t[..., -1]
    w_write = jnp.exp(b[..., -1:] - b + log_i - m_new[..., None])
    w_carry = jnp.exp(b[..., -1] + m_state - m_new)
    c_new = w_carry[..., None, None] * c_state + jnp.einsum("bhs,bhsk,bhsv->bhkv", w_write, k, v)
    n_new = w_carry[..., None] * n_state + jnp.einsum("bhs,bhsk->bhk", w_write, k)
    return (c_new, n_new, m_new), h


def mlstm_mixer(q_pre, k_pre, v_pre, o_pre, ig_pre, fg_pre, conv_w, conv_b, b_i, b_f, norm_g):
    B = q_pre.shape[0]
    f32 = jnp.float32
    qk = jax.nn.silu(causal_depthwise_conv(jnp.concatenate([q_pre, k_pre], -1), conv_w, conv_b).astype(f32))
    q, k = jnp.split(qk, 2, axis=-1)
    k = k * (ML_DQK ** -0.5)
    v = v_pre.astype(f32)
    log_i = ig_pre.astype(f32) + b_i.astype(f32)
    log_f = jax.nn.log_sigmoid(fg_pre.astype(f32) + b_f.astype(f32))
    carry0 = (jnp.zeros((B, ML_HEADS, ML_DQK, ML_DV), f32), jnp.zeros((B, ML_HEADS, ML_DQK), f32),
              jnp.zeros((B, ML_HEADS), f32))
    _, h = lax.scan(mlstm_chunk_step, carry0,
                    (to_chunks(q, ML_HEADS, ML_CHUNK), to_chunks(k, ML_HEADS, ML_CHUNK),
                     to_chunks(v, ML_HEADS, ML_CHUNK), scalars_to_chunks(log_i, ML_CHUNK),
                     scalars_to_chunks(log_f, ML_CHUNK)))
    h = from_chunks(h)
    return jax.nn.sigmoid(o_pre.astype(f32)) * head_rms_norm(h, norm_g, ML_HEADS)


def swiglu(x, w_gate, w_up, w_down):
    return (jax.nn.silu(x @ w_gate) * (x @ w_up)) @ w_down


def moe_swiglu(x, w_router, w_gate, w_up, w_down):
    B, S, D = x.shape
    T = B * S
    h = x.reshape(T, D)
    logits = (h @ w_router).astype(jnp.float32)
    top_logits, top_idx = lax.top_k(logits, TOP_K)
    gates = jax.nn.softmax(top_logits, axis=-1)
    n_assign = T * TOP_K
    expert = top_idx.reshape(-1).astype(jnp.int32)
    token = jnp.arange(n_assign, dtype=jnp.int32) // TOP_K
    gate = gates.reshape(-1)
    order = jnp.argsort(expert)
    expert, token, gate = expert[order], token[order], gate[order]
    counts = jnp.bincount(expert, length=N_EXPERTS)
    padded = (counts + MOE_BLOCK - 1) // MOE_BLOCK * MOE_BLOCK
    padded_end = jnp.cumsum(padded)
    group_start = jnp.cumsum(counts) - counts
    dest = (padded_end - padded)[expert] + jnp.arange(n_assign, dtype=jnp.int32) - group_start[expert]
    n_blocks = -(-(n_assign + N_EXPERTS * (MOE_BLOCK - 1)) // MOE_BLOCK)
    n_slots = n_blocks * MOE_BLOCK
    slot_token = jnp.zeros((n_slots,), jnp.int32).at[dest].set(token)
    slot_gate = jnp.zeros((n_slots,), jnp.float32).at[dest].set(gate)
    block_expert = jnp.minimum(
        jnp.searchsorted(padded_end, jnp.arange(n_blocks, dtype=jnp.int32) * MOE_BLOCK, side="right"),
        N_EXPERTS - 1)
    xs = h[slot_token].reshape(n_blocks, MOE_BLOCK, D)

    def expert_block(args):
        xb, e = args
        return (jax.nn.silu(xb @ w_gate[e]) * (xb @ w_up[e])) @ w_down[e]

    ys = lax.map(expert_block, (xs, block_expert)).reshape(n_slots, D)
    ys = ys * slot_gate[:, None].astype(ys.dtype)
    out = jnp.zeros_like(h).at[slot_token].add(ys)
    return out.reshape(B, S, D)


def setup_inputs(seed: int = 0) -> dict:
    key = jax.random.key(seed)
    ks = jax.random.split(key, 21)
    f32 = jnp.float32

    def nrm(k, shape, scale):
        return scale * jax.random.normal(k, shape, f32)

    in_cols = sum(IN_SPLITS)
    return {
        "x": nrm(ks[0], (BATCH, SEQ, D_MODEL), 1.0),
        "w_in": nrm(ks[1], (DEPTH, D_MODEL, in_cols), D_MODEL ** -0.5),
        "w_out": nrm(ks[2], (DEPTH, D_MIX, D_MODEL), BETA * D_MIX ** -0.5),
        "hg_lb_logits": nrm(ks[3], (DEPTH, HG_WIDTH), 1.0),
        "hg_norm_g": 1.0 + nrm(ks[4], (DEPTH, HG_WIDTH), 0.02),
        "ml_conv_w": nrm(ks[5], (DEPTH, CONV_W, 2 * ML_QK), CONV_W ** -0.5),
        "ml_conv_b": nrm(ks[6], (DEPTH, 2 * ML_QK), 0.02),
        "ml_b_i": nrm(ks[7], (DEPTH, ML_HEADS), 0.1),
        "ml_b_f": jnp.linspace(3.0, 6.0, ML_HEADS, dtype=f32) + nrm(ks[8], (DEPTH, ML_HEADS), 0.1),
        "ml_norm_g": 1.0 + nrm(ks[9], (DEPTH, ML_WIDTH), 0.02),
        "ln_mix_g": 1.0 + nrm(ks[10], (DEPTH, D_MODEL), 0.02),
        "ln_mix_b": nrm(ks[11], (DEPTH, D_MODEL), 0.02),
        "ln_ffn_g": 1.0 + nrm(ks[12], (DEPTH, D_MODEL), 0.02),
        "ln_ffn_b": nrm(ks[13], (DEPTH, D_MODEL), 0.02),
        "ffn_w_gate": nrm(ks[14], (N_DENSE_LAYERS, D_MODEL, FFN_DENSE), D_MODEL ** -0.5),
        "ffn_w_up": nrm(ks[15], (N_DENSE_LAYERS, D_MODEL, FFN_DENSE), BETA * D_MODEL ** -0.5),
        "ffn_w_down": nrm(ks[16], (N_DENSE_LAYERS, FFN_DENSE, D_MODEL), BETA * FFN_DENSE ** -0.5),
        "moe_w_router": nrm(ks[17], (N_MOE_LAYERS, D_MODEL, N_EXPERTS), D_MODEL ** -0.5),
        "moe_w_gate": nrm(ks[18], (N_MOE_LAYERS, N_EXPERTS, D_MODEL, FFN_EXPERT), D_MODEL ** -0.5),
        "moe_w_up": nrm(ks[19], (N_MOE_LAYERS, N_EXPERTS, D_MODEL, FFN_EXPERT), BETA * D_MODEL ** -0.5),
        "moe_w_down": nrm(ks[20], (N_MOE_LAYERS, N_EXPERTS, FFN_EXPERT, D_MODEL), BETA * FFN_EXPERT ** -0.5),
    }


def reference(x, w_in, w_out, hg_lb_logits, hg_norm_g, ml_conv_w, ml_conv_b, ml_b_i, ml_b_f, ml_norm_g,
              ln_mix_g, ln_mix_b, ln_ffn_g, ln_ffn_b, ffn_w_gate, ffn_w_up, ffn_w_down,
              moe_w_router, moe_w_gate, moe_w_up, moe_w_down):
    p = jax.nn.softmax(hg_lb_logits.astype(jnp.float32), axis=0)
    lower_bounds = jnp.cumsum(p, axis=0) - p[0]
    split_points = [int(s) for s in np.cumsum(IN_SPLITS)[:-1]]
    for layer in range(DEPTH):
        proj = x @ w_in[layer]
        hg_q, hg_f, hg_i, hg_g, ml_q, ml_k, ml_v, ml_o, ml_ig, ml_fg = jnp.split(proj, split_points, axis=-1)
        hg_out = hgrn2_mixer(hg_q, hg_f, hg_i, hg_g, lower_bounds[layer], hg_norm_g[layer])
        ml_out = mlstm_mixer(ml_q, ml_k, ml_v, ml_o, ml_ig, ml_fg, ml_conv_w[layer], ml_conv_b[layer],
                             ml_b_i[layer], ml_b_f[layer], ml_norm_g[layer])
        mixed = jnp.concatenate([hg_out, ml_out], axis=-1).astype(x.dtype) @ w_out[layer]
        x = layer_norm(ALPHA * x + mixed, ln_mix_g[layer], ln_mix_b[layer])
        if layer % 2 == 0:
            j = layer // 2
            ff = swiglu(x, ffn_w_gate[j], ffn_w_up[j], ffn_w_down[j])
        else:
            j = layer // 2
            ff = moe_swiglu(x, moe_w_router[j], moe_w_gate[j], moe_w_up[j], moe_w_down[j])
        x = layer_norm(ALPHA * x + ff.astype(x.dtype), ln_ffn_g[layer], ln_ffn_b[layer])
    return x
```

```python
import functools

import jax
import jax.numpy as jnp
from jax import lax
from jax.experimental import pallas as pl
from jax.experimental.pallas import tpu as pltpu

F32 = jnp.float32
BF16 = jnp.bfloat16
HIGHEST = lax.Precision.HIGHEST

MODEL_DEPTH = 4
HG_HEADS = 8
HG_D = 128
ML_HEADS = 4
ML_DQK = 128
ML_DV = 256
CONV_W = 4
N_EXPERTS = 8
TOP_K = 2
ALPHA = (2 * MODEL_DEPTH) ** 0.25
LN_EPS = 1e-5
RMS_EPS = 1e-6
NEG_BIG = -1e30
F_MIN = 1e-30

LANES = 128
SUBLANES = 8
VMEM_LIMIT_BYTES = 56 * 1024 * 1024

CHUNK = 64
SUB = 16
CONV_PAD = SUBLANES


def _cparams(*sem):
    return pltpu.CompilerParams(dimension_semantics=sem, vmem_limit_bytes=VMEM_LIMIT_BYTES)


def _sigmoid(x):
    return 1.0 / (1.0 + jnp.exp(-x))


def _layer_norm(y, g, b):
    mu = jnp.mean(y, axis=-1, keepdims=True)
    yc = y - mu
    var = jnp.mean(yc * yc, axis=-1, keepdims=True)
    return yc * lax.rsqrt(var + LN_EPS) * g + b


def _proj_kernel(x_ref, w_ref, o_ref):
    o_ref[...] = jnp.dot(x_ref[...].astype(BF16), w_ref[...], preferred_element_type=F32)


def _proj(x, w, tm, tn):
    T, D = x.shape
    N = w.shape[1]
    return pl.pallas_call(
        _proj_kernel,
        grid=(T // tm, N // tn),
        in_specs=[pl.BlockSpec((tm, D), lambda i, j: (i, 0)),
                  pl.BlockSpec((D, tn), lambda i, j: (0, j))],
        out_specs=pl.BlockSpec((tm, tn), lambda i, j: (i, j)),
        out_shape=jax.ShapeDtypeStruct((T, N), F32),
        compiler_params=_cparams("parallel", "arbitrary"),
        name="in_proj",
    )(x, w)


def _hgrn2_kernel(q_ref, f_ref, i_ref, g_ref, lb_ref, ng_ref, o_ref,
                  st_ref, kpad_ref, fpad_ref, prod_ref, *, n_chunks):
    @pl.when(pl.program_id(2) == 0)
    def _():
        st_ref[...] = jnp.zeros_like(st_ref)

    kpad_ref[0:SUB, :] = jnp.zeros((SUB, HG_D), F32)
    fpad_ref[0:SUB, :] = jnp.zeros((SUB, HG_D), F32)

    lb = lb_ref[...]
    one_m_lb = 1.0 - lb
    ng = ng_ref[...]

    row = lax.broadcasted_iota(jnp.int32, (CHUNK, CHUNK), 0)
    col = lax.broadcasted_iota(jnp.int32, (CHUNK, CHUNK), 1)
    tril = (col <= row).astype(F32)
    same_blk = (row // SUB) == (col // SUB)
    band = jnp.where(same_blk & (col <= row), row - col, -1)
    ones_bf = jnp.ones((HG_D, CHUNK), BF16)
    n_sub = CHUNK // SUB
    strip_col = lax.broadcasted_iota(jnp.int32, (SUB, CHUNK), 1)

    for c in range(n_chunks):
        r0 = c * CHUNK
        qp = q_ref[r0:r0 + CHUNK, :]
        fp = f_ref[r0:r0 + CHUNK, :]
        v = i_ref[r0:r0 + CHUNK, :]
        gp = g_ref[r0:r0 + CHUNK, :]

        q = qp * _sigmoid(qp)
        fc = jnp.maximum(lb + one_m_lb * _sigmoid(fp), F_MIN)
        log_f = jnp.log(fc)
        k = one_m_lb * _sigmoid(-fp)
        b = jnp.dot(tril, log_f, precision=HIGHEST, preferred_element_type=F32)
        b_last = b[CHUNK - 1:CHUNK, :]

        v_bf = v.astype(BF16)
        st = st_ref[...]

        qd = q * jnp.exp(b)
        o = lax.dot_general(qd.astype(BF16), st.astype(BF16), (((1,), (1,)), ((), ())),
                            preferred_element_type=F32)
        kd = k * jnp.exp(b_last - b)
        st_ref[...] = st * jnp.exp(b_last) + jnp.dot(v.T.astype(BF16), kd.astype(BF16),
                                                      preferred_element_type=F32)

        strips = [jnp.zeros((SUB, CHUNK), F32)]
        for i in range(1, n_sub):
            lo = i * SUB
            ref_i = b[lo - 1:lo, :]
            qt = q[lo:lo + SUB, :] * jnp.exp(b[lo:lo + SUB, :] - ref_i)
            kt = k * jnp.exp(jnp.minimum(ref_i - b, 0.0))
            s_i = lax.dot_general(qt.astype(BF16), kt.astype(BF16), (((1,), (1,)), ((), ())),
                                  preferred_element_type=F32)
            strips.append(jnp.where(strip_col < lo, s_i, 0.0))
        p = jnp.concatenate(strips, axis=0)

        kpad_ref[SUB:SUB + CHUNK, :] = k
        fpad_ref[SUB:SUB + CHUNK, :] = fc
        prod_ref[0:CHUNK, :] = (q * k).astype(BF16)
        e = None
        for d in range(1, SUB):
            f_sh = fpad_ref[SUB - (d - 1):SUB - (d - 1) + CHUNK, :]
            e = f_sh if e is None else e * f_sh
            k_sh = kpad_ref[SUB - d:SUB - d + CHUNK, :]
            prod_ref[d * CHUNK:(d + 1) * CHUNK, :] = (q * e * k_sh).astype(BF16)
        rsum = jnp.dot(prod_ref[...], ones_bf, preferred_element_type=F32)
        for d in range(SUB):
            p = jnp.where(band == d, rsum[d * CHUNK:(d + 1) * CHUNK, :], p)

        o = o + jnp.dot(p.astype(BF16), v_bf, preferred_element_type=F32)

        on = o * lax.rsqrt(jnp.mean(o * o, axis=-1, keepdims=True) + RMS_EPS) * ng
        o_ref[r0:r0 + CHUNK, :] = (on * (gp * _sigmoid(gp))).astype(o_ref.dtype)


def _hgrn2(proj, lb, ng, B, S, rows):
    T = B * S
    ns = S // rows
    n_chunks = rows // CHUNK
    H = HG_HEADS

    def col_spec(group):
        return pl.BlockSpec((rows, HG_D), lambda b, h, j, g=group: (b * ns + j, g * H + h))

    vec_spec = pl.BlockSpec((1, HG_D), lambda b, h, j: (0, h))
    return pl.pallas_call(
        functools.partial(_hgrn2_kernel, n_chunks=n_chunks),
        grid=(B, H, ns),
        in_specs=[col_spec(0), col_spec(1), col_spec(2), col_spec(3), vec_spec, vec_spec],
        out_specs=pl.BlockSpec((rows, HG_D), lambda b, h, j: (b * ns + j, h)),
        out_shape=jax.ShapeDtypeStruct((T, H * HG_D), BF16),
        scratch_shapes=[pltpu.VMEM((HG_D, HG_D), F32),
                        pltpu.VMEM((SUB + CHUNK, HG_D), F32),
                        pltpu.VMEM((SUB + CHUNK, HG_D), F32),
                        pltpu.VMEM((SUB * CHUNK, HG_D), BF16)],
        compiler_params=_cparams("parallel", "parallel", "arbitrary"),
        name="hgrn2",
    )(proj, proj, proj, proj, lb, ng)


def _mlstm_kernel(q_ref, k_ref, v_ref, og_ref, gt_ref, cw_ref, cb_ref, gb_ref, ng_ref, o_ref,
                  cq_ref, ck_ref, c_ref, n_ref, m_ref, *, n_chunks):
    rows = n_chunks * CHUNK
    W = ML_HEADS * ML_DQK

    @pl.when(pl.program_id(1) == 0)
    def _():
        cq_ref[0:CONV_PAD, :] = jnp.zeros((CONV_PAD, W), F32)
        ck_ref[0:CONV_PAD, :] = jnp.zeros((CONV_PAD, W), F32)
        c_ref[...] = jnp.zeros_like(c_ref)
        n_ref[...] = jnp.zeros_like(n_ref)
        m_ref[...] = jnp.zeros_like(m_ref)

    cq_ref[CONV_PAD:CONV_PAD + rows, :] = q_ref[...]
    ck_ref[CONV_PAD:CONV_PAD + rows, :] = k_ref[...]

    row = lax.broadcasted_iota(jnp.int32, (CHUNK, CHUNK), 0)
    col = lax.broadcasted_iota(jnp.int32, (CHUNK, CHUNK), 1)
    causal = col <= row
    tril = causal.astype(F32)
    lane = lax.broadcasted_iota(jnp.int32, (CHUNK, LANES), 1)
    k_scale = ML_DQK ** -0.5

    for c in range(n_chunks):
        r0 = c * CHUNK
        pre = gt_ref[r0:r0 + CHUNK, :] + gb_ref[...]
        log_f = jnp.minimum(pre, 0.0) - jnp.log1p(jnp.exp(-jnp.abs(pre)))
        bcum = jnp.dot(tril, log_f, precision=HIGHEST, preferred_element_type=F32)
        x = jnp.where(lane < ML_HEADS, pre, bcum)
        xt = x.T

        for h in range(ML_HEADS):
            cs = slice(h * ML_DQK, (h + 1) * ML_DQK)
            vs = slice(h * ML_DV, (h + 1) * ML_DV)

            def conv(src_ref, w_off):
                acc = cb_ref[0:1, w_off + h * ML_DQK:w_off + (h + 1) * ML_DQK]
                for j in range(CONV_W):
                    start = CONV_PAD + r0 - (CONV_W - 1) + j
                    acc = acc + (cw_ref[j:j + 1, w_off + h * ML_DQK:w_off + (h + 1) * ML_DQK]
                                 * src_ref[start:start + CHUNK, cs])
                return acc * _sigmoid(acc)

            q = conv(cq_ref, 0)
            k = conv(ck_ref, W) * k_scale
            v = v_ref[r0:r0 + CHUNK, vs]
            q_bf = q.astype(BF16)
            v_bf = v.astype(BF16)

            li_c = x[:, h:h + 1]
            b_c = x[:, ML_HEADS + h:ML_HEADS + h + 1]
            li_r = xt[h:h + 1, :]
            b_r = xt[ML_HEADS + h:ML_HEADS + h + 1, :]
            b_last = b_c[CHUNK - 1:CHUNK, :]
            m_prev = m_ref[h][0:1, 0:1]
            c_state = c_ref[h]
            n_state = n_ref[h][0:1, :]

            log_intra = jnp.where(causal, b_c - b_r + li_r, NEG_BIG)
            log_inter = b_c + m_prev
            m_t = jnp.maximum(log_inter, jnp.max(log_intra, axis=-1, keepdims=True))
            w_intra = jnp.exp(log_intra - m_t)
            w_inter = jnp.exp(log_inter - m_t)
            scores = lax.dot_general(q_bf, k.astype(BF16), (((1,), (1,)), ((), ())),
                                     preferred_element_type=F32) * w_intra
            num = (jnp.dot(scores.astype(BF16), v_bf, preferred_element_type=F32)
                   + w_inter * jnp.dot(q_bf, c_state.astype(BF16), preferred_element_type=F32))
            den = (jnp.sum(scores, axis=-1, keepdims=True)
                   + w_inter * jnp.sum(q * n_state, axis=-1, keepdims=True))
            hh = num / jnp.maximum(jnp.abs(den), jnp.exp(-m_t))

            m_new = m_t[CHUNK - 1:CHUNK, :]
            w_write = jnp.exp(b_last - b_c + li_c - m_new)
            w_carry = jnp.exp(b_last + m_prev - m_new)
            kw = k * w_write
            c_ref[h] = w_carry * c_state + jnp.dot(kw.T.astype(BF16), v_bf,
                                                   preferred_element_type=F32)
            n_new = w_carry * n_state + jnp.sum(kw, axis=0, keepdims=True)
            n_ref[h] = jnp.broadcast_to(n_new, (SUBLANES, ML_DQK))
            m_ref[h] = jnp.broadcast_to(m_new, (SUBLANES, LANES))

            hn = hh * lax.rsqrt(jnp.mean(hh * hh, axis=-1, keepdims=True) + RMS_EPS) * ng_ref[0:1, vs]
            og = og_ref[r0:r0 + CHUNK, vs]
            o_ref[r0:r0 + CHUNK, vs] = (_sigmoid(og) * hn).astype(o_ref.dtype)

    cq_ref[0:CONV_PAD, :] = cq_ref[rows:rows + CONV_PAD, :]
    ck_ref[0:CONV_PAD, :] = ck_ref[rows:rows + CONV_PAD, :]


def _mlstm(proj, gates, conv_w, conv_b, gate_bias, ng, B, S, rows, col0):
    T = B * S
    ns = S // rows
    n_chunks = rows // CHUNK
    W = ML_HEADS * ML_DQK
    V = ML_HEADS * ML_DV
    qb = col0 // W
    vb = (col0 + 2 * W) // V

    def rowmap(cb):
        return lambda b, j, cb=cb: (b * ns + j, cb)

    const = lambda b, j: (0, 0)
    return pl.pallas_call(
        functools.partial(_mlstm_kernel, n_chunks=n_chunks),
        grid=(B, ns),
        in_specs=[pl.BlockSpec((rows, W), rowmap(qb)),
                  pl.BlockSpec((rows, W), rowmap(qb + 1)),
                  pl.BlockSpec((rows, V), rowmap(vb)),
                  pl.BlockSpec((rows, V), rowmap(vb + 1)),
                  pl.BlockSpec((rows, LANES), rowmap(0)),
                  pl.BlockSpec((CONV_W, 2 * W), const),
                  pl.BlockSpec((1, 2 * W), const),
                  pl.BlockSpec((1, LANES), const),
                  pl.BlockSpec((1, V), const)],
        out_specs=pl.BlockSpec((rows, V), rowmap(0)),
        out_shape=jax.ShapeDtypeStruct((T, V), BF16),
        scratch_shapes=[pltpu.VMEM((rows + 2 * CONV_PAD, W), F32),
                        pltpu.VMEM((rows + 2 * CONV_PAD, W), F32),
                        pltpu.VMEM((ML_HEADS, ML_DQK, ML_DV), F32),
                        pltpu.VMEM((ML_HEADS, SUBLANES, ML_DQK), F32),
                        pltpu.VMEM((ML_HEADS, SUBLANES, LANES), F32)],
        compiler_params=_cparams("parallel", "arbitrary"),
        name="mlstm",
    )(proj, proj, proj, proj, gates, conv_w, conv_b, gate_bias, ng)


def _top2(logits):
    lane = lax.broadcasted_iota(jnp.int32, logits.shape, 1)
    lg = jnp.where(lane < N_EXPERTS, logits, -jnp.inf)
    m0 = jnp.max(lg, axis=-1, keepdims=True)
    i0 = jnp.min(jnp.where(lg == m0, lane, LANES), axis=-1, keepdims=True)
    lg1 = jnp.where(lane == i0, -jnp.inf, lg)
    m1 = jnp.max(lg1, axis=-1, keepdims=True)
    i1 = jnp.min(jnp.where(lg1 == m1, lane, LANES), axis=-1, keepdims=True)
    e1 = jnp.exp(m1 - m0)
    g0 = 1.0 / (1.0 + e1)
    g1 = e1 / (1.0 + e1)
    idx = jnp.where(lane == 0, i0, jnp.where(lane == 1, i1, 0))
    gate = jnp.where(lane == 0, g0, jnp.where(lane == 1, g1, 0.0))
    return idx, gate


def _outproj_kernel(hg_ref, ml_ref, w_ref, x_ref, g_ref, b_ref, *rest, route):
    if route:
        wr_ref, o_ref, idx_ref, gate_ref = rest
    else:
        (o_ref,) = rest
    half = hg_ref.shape[1]
    mixed = (jnp.dot(hg_ref[...], w_ref[0:half, :], preferred_element_type=F32)
             + jnp.dot(ml_ref[...], w_ref[half:, :], preferred_element_type=F32))
    y = _layer_norm(ALPHA * x_ref[...] + mixed, g_ref[...], b_ref[...])
    o_ref[...] = y
    if route:
        logits = jnp.dot(y, wr_ref[...], precision=HIGHEST, preferred_element_type=F32)
        idx, gate = _top2(logits)
        idx_ref[...] = idx
        gate_ref[...] = gate


def _outproj(hg, ml, w, x, g, b, w_router, tm):
    T, D = x.shape
    half = hg.shape[1]
    route = w_router is not None
    row = lambda i: (i, 0)
    const = lambda i: (0, 0)
    in_specs = [pl.BlockSpec((tm, half), row), pl.BlockSpec((tm, half), row),
                pl.BlockSpec((D, D), const), pl.BlockSpec((tm, D), row),
                pl.BlockSpec((1, D), const), pl.BlockSpec((1, D), const)]
    out_specs = [pl.BlockSpec((tm, D), row)]
    out_shape = [jax.ShapeDtypeStruct((T, D), F32)]
    args = [hg, ml, w, x, g, b]
    if route:
        in_specs.append(pl.BlockSpec((D, LANES), const))
        out_specs += [pl.BlockSpec((tm, LANES), row), pl.BlockSpec((tm, LANES), row)]
        out_shape += [jax.ShapeDtypeStruct((T, LANES), jnp.int32),
                      jax.ShapeDtypeStruct((T, LANES), F32)]
        args.append(w_router)
    return pl.pallas_call(
        functools.partial(_outproj_kernel, route=route),
        grid=(T // tm,),
        in_specs=in_specs, out_specs=out_specs, out_shape=out_shape,
        compiler_params=_cparams("parallel"),
        name="out_proj_route" if route else "out_proj",
    )(*args)


def _ffn_kernel(x_ref, wg_ref, wu_ref, wd_ref, g_ref, b_ref, o_ref, xb_ref, acc_ref):
    f = pl.program_id(1)

    @pl.when(f == 0)
    def _():
        xb_ref[...] = x_ref[...].astype(BF16)
        acc_ref[...] = jnp.zeros_like(acc_ref)

    xb = xb_ref[...]
    gt = jnp.dot(xb, wg_ref[...], preferred_element_type=F32)
    up = jnp.dot(xb, wu_ref[...], preferred_element_type=F32)
    h = (gt * _sigmoid(gt) * up).astype(BF16)
    acc_ref[...] += jnp.dot(h, wd_ref[...], preferred_element_type=F32)

    @pl.when(f == pl.num_programs(1) - 1)
    def _():
        o_ref[...] = _layer_norm(ALPHA * x_ref[...] + acc_ref[...], g_ref[...], b_ref[...])


def _ffn(x, wg, wu, wd, g, b, tm, tf):
    T, D = x.shape
    Fp = wg.shape[1]
    return pl.pallas_call(
        _ffn_kernel,
        grid=(T // tm, Fp // tf),
        in_specs=[pl.BlockSpec((tm, D), lambda i, f: (i, 0)),
                  pl.BlockSpec((D, tf), lambda i, f: (0, f)),
                  pl.BlockSpec((D, tf), lambda i, f: (0, f)),
                  pl.BlockSpec((tf, D), lambda i, f: (f, 0)),
                  pl.BlockSpec((1, D), lambda i, f: (0, 0)),
                  pl.BlockSpec((1, D), lambda i, f: (0, 0))],
        out_specs=pl.BlockSpec((tm, D), lambda i, f: (i, 0)),
        out_shape=jax.ShapeDtypeStruct((T, D), F32),
        scratch_shapes=[pltpu.VMEM((tm, D), BF16), pltpu.VMEM((tm, D), F32)],
        compiler_params=_cparams("parallel", "arbitrary"),
        name="dense_ffn",
    )(x, wg, wu, wd, g, b)


def _moe_kernel(be_ref, tok_ref, x_hbm, wg_ref, wu_ref, wd_ref, o_ref,
                xg_ref, xb_ref, acc_ref, sem, *, blk):
    del be_ref
    f = pl.program_id(1)

    def row_copy(r):
        return pltpu.make_async_copy(x_hbm.at[pl.ds(tok_ref[0, 0, r], 1), :],
                                     xg_ref.at[pl.ds(r, 1), :], sem)

    @pl.when(f == 0)
    def _():
        def start(r, c):
            row_copy(r).start()
            return c
        lax.fori_loop(0, blk, start, 0, unroll=8)

        def wait(r, c):
            row_copy(r).wait()
            return c
        lax.fori_loop(0, blk, wait, 0, unroll=8)
        xb_ref[...] = xg_ref[...].astype(BF16)
        acc_ref[...] = jnp.zeros_like(acc_ref)

    xb = xb_ref[...]
    gt = jnp.dot(xb, wg_ref[...], preferred_element_type=F32)
    up = jnp.dot(xb, wu_ref[...], preferred_element_type=F32)
    h = (gt * _sigmoid(gt) * up).astype(BF16)
    acc_ref[...] += jnp.dot(h, wd_ref[...], preferred_element_type=F32)

    @pl.when(f == pl.num_programs(1) - 1)
    def _():
        o_ref[...] = acc_ref[...]


def _moe_ffn(block_expert, slot_token, x, wg, wu, wd, blk, tf):
    T, D = x.shape
    F = wg.shape[2]
    n_blocks = block_expert.shape[0]
    tok3 = slot_token.reshape(n_blocks, 1, blk)
    grid_spec = pltpu.PrefetchScalarGridSpec(
        num_scalar_prefetch=1,
        grid=(n_blocks, F // tf),
        in_specs=[pl.BlockSpec((1, 1, blk), lambda i, f, be: (i, 0, 0), memory_space=pltpu.SMEM),
                  pl.BlockSpec(memory_space=pl.ANY),
                  pl.BlockSpec((None, D, tf), lambda i, f, be: (be[i], 0, f)),
                  pl.BlockSpec((None, D, tf), lambda i, f, be: (be[i], 0, f)),
                  pl.BlockSpec((None, tf, D), lambda i, f, be: (be[i], f, 0))],
        out_specs=pl.BlockSpec((blk, D), lambda i, f, be: (i, 0)),
        scratch_shapes=[pltpu.VMEM((blk, D), F32), pltpu.VMEM((blk, D), BF16),
                        pltpu.VMEM((blk, D), F32), pltpu.SemaphoreType.DMA(())],
    )
    return pl.pallas_call(
        functools.partial(_moe_kernel, blk=blk),
        grid_spec=grid_spec,
        out_shape=jax.ShapeDtypeStruct((n_blocks * blk, D), F32),
        compiler_params=_cparams("arbitrary", "arbitrary"),
        name="moe_ffn",
    )(block_expert, tok3, x, wg, wu, wd)


def _combine_kernel(dest_ref, ys_hbm, x_ref, gate_ref, g_ref, b_ref, o_ref, buf_ref, sem, *, tm):
    def row_copy(r):
        return pltpu.make_async_copy(ys_hbm.at[pl.ds(dest_ref[0, 0, r], 1), :],
                                     buf_ref.at[pl.ds(r, 1), :], sem)

    def start(r, c):
        row_copy(r).start()
        return c
    lax.fori_loop(0, TOP_K * tm, start, 0, unroll=8)

    def wait(r, c):
        row_copy(r).wait()
        return c
    lax.fori_loop(0, TOP_K * tm, wait, 0, unroll=8)

    gate = gate_ref[...]
    ff = buf_ref[0:tm, :] * gate[:, 0:1] + buf_ref[tm:2 * tm, :] * gate[:, 1:2]
    o_ref[...] = _layer_norm(ALPHA * x_ref[...] + ff, g_ref[...], b_ref[...])


def _combine(dest, ys, x, gate, g, b, tm):
    T, D = x.shape
    nt = T // tm
    dest3 = dest.reshape(nt, tm, TOP_K).transpose(0, 2, 1).reshape(nt, 1, TOP_K * tm)
    return pl.pallas_call(
        functools.partial(_combine_kernel, tm=tm),
        grid=(nt,),
        in_specs=[pl.BlockSpec((1, 1, TOP_K * tm), lambda i: (i, 0, 0), memory_space=pltpu.SMEM),
                  pl.BlockSpec(memory_space=pl.ANY),
                  pl.BlockSpec((tm, D), lambda i: (i, 0)),
                  pl.BlockSpec((tm, LANES), lambda i: (i, 0)),
                  pl.BlockSpec((1, D), lambda i: (0, 0)),
                  pl.BlockSpec((1, D), lambda i: (0, 0))],
        out_specs=pl.BlockSpec((tm, D), lambda i: (i, 0)),
        out_shape=jax.ShapeDtypeStruct((T, D), F32),
        scratch_shapes=[pltpu.VMEM((TOP_K * tm, D), F32), pltpu.SemaphoreType.DMA(())],
        compiler_params=_cparams("arbitrary"),
        name="moe_combine",
    )(dest3, ys, x, gate, g, b)


def _dispatch(idx2, blk, n_blocks):
    T = idx2.shape[0]
    n_assign = T * TOP_K
    expert = idx2.reshape(-1)
    onehot = (expert[:, None] == jnp.arange(N_EXPERTS, dtype=jnp.int32)[None, :]).astype(jnp.int32)
    csum = jnp.cumsum(onehot, axis=0)
    rank = jnp.sum(onehot * (csum - 1), axis=1)
    counts = csum[-1]
    padded = (counts + blk - 1) // blk * blk
    padded_end = jnp.cumsum(padded)
    dest = (padded_end - padded)[expert] + rank
    token = jnp.arange(n_assign, dtype=jnp.int32) // TOP_K
    slot_token = jnp.zeros((n_blocks * blk,), jnp.int32).at[dest].set(token)
    block_expert = jnp.minimum(
        jnp.searchsorted(padded_end, jnp.arange(n_blocks, dtype=jnp.int32) * blk, side="right"),
        N_EXPERTS - 1).astype(jnp.int32)
    return dest.reshape(T, TOP_K).astype(jnp.int32), slot_token, block_expert


def _tile(n, pref):
    t = min(n, pref)
    while n % t:
        t //= 2
    return t


def kernel(x, w_in, w_out, hg_lb_logits, hg_norm_g, ml_conv_w, ml_conv_b, ml_b_i, ml_b_f, ml_norm_g,
           ln_mix_g, ln_mix_b, ln_ffn_g, ln_ffn_b, ffn_w_gate, ffn_w_up, ffn_w_down,
           moe_w_router, moe_w_gate, moe_w_up, moe_w_down):
    B, S, D = x.shape
    T = B * S
    depth = w_in.shape[0]
    hg_w = HG_HEADS * HG_D
    main_cols = 4 * hg_w + 2 * ML_HEADS * ML_DQK + 2 * ML_HEADS * ML_DV

    p = jax.nn.softmax(hg_lb_logits.astype(F32), axis=0)
    lower_bounds = jnp.cumsum(p, axis=0) - p[0]

    rows = _tile(S, 256)
    tm_proj = _tile(T, 1024)
    tm = _tile(T, 512)
    blk = _tile(T, 512)
    n_blocks = -(-(T * TOP_K + N_EXPERTS * (blk - 1)) // blk)

    f_dense = ffn_w_gate.shape[2]
    tf_dense = 512
    f_pad = -(-f_dense // tf_dense) * tf_dense - f_dense

    xt = x.reshape(T, D).astype(F32)
    for layer in range(depth):
        w_main = w_in[layer, :, :main_cols].astype(BF16)
        w_gates = jnp.pad(w_in[layer, :, main_cols:], ((0, 0), (0, LANES - 2 * ML_HEADS))).astype(BF16)
        proj = _proj(xt, w_main, tm_proj, 1024)
        gates = _proj(xt, w_gates, tm_proj, LANES)

        hg = _hgrn2(proj, lower_bounds[layer][None, :], hg_norm_g[layer][None, :].astype(F32), B, S, rows)
        gate_bias = jnp.pad(jnp.concatenate([ml_b_i[layer], ml_b_f[layer]]).astype(F32),
                            (0, LANES - 2 * ML_HEADS))[None, :]
        ml = _mlstm(proj, gates, ml_conv_w[layer].astype(F32), ml_conv_b[layer][None, :].astype(F32),
                    gate_bias, ml_norm_g[layer][None, :].astype(F32), B, S, rows, 4 * hg_w)

        moe = layer % 2 == 1
        j = layer // 2
        w_router = None
        if moe:
            w_router = jnp.pad(moe_w_router[j].astype(F32), ((0, 0), (0, LANES - N_EXPERTS)))
        res = _outproj(hg, ml, w_out[layer].astype(BF16), xt,
                       ln_mix_g[layer][None, :], ln_mix_b[layer][None, :], w_router, tm)
        g2 = ln_ffn_g[layer][None, :]
        b2 = ln_ffn_b[layer][None, :]
        if not moe:
            (x1,) = res
            wg = jnp.pad(ffn_w_gate[j].astype(BF16), ((0, 0), (0, f_pad)))
            wu = jnp.pad(ffn_w_up[j].astype(BF16), ((0, 0), (0, f_pad)))
            wd = jnp.pad(ffn_w_down[j].astype(BF16), ((0, f_pad), (0, 0)))
            xt = _ffn(x1, wg, wu, wd, g2, b2, tm, tf_dense)
        else:
            x1, idx, gate = res
            dest, slot_token, block_expert = _dispatch(idx[:, :TOP_K], blk, n_blocks)
            ys = _moe_ffn(block_expert, slot_token, x1, moe_w_gate[j].astype(BF16),
                          moe_w_up[j].astype(BF16), moe_w_down[j].astype(BF16), blk, 1024)
            xt = _combine(dest, ys, x1, gate, g2, b2, _tile(T, 256))
    return xt.reshape(B, S, D).astype(x.dtype)
```

```python
import functools

import jax
import jax.numpy as jnp
import numpy as np
from jax import lax
from jax.experimental import pallas as pl
from jax.experimental.pallas import tpu as pltpu

F32 = jnp.float32
BF16 = jnp.bfloat16
HIGHEST = lax.Precision.HIGHEST

MODEL_DEPTH = 4
HG_HEADS = 8
HG_D = 128
ML_HEADS = 4
ML_DQK = 128
ML_DV = 256
CONV_W = 4
N_EXPERTS = 8
TOP_K = 2
ALPHA = (2 * MODEL_DEPTH) ** 0.25
LN_EPS = 1e-5
RMS_EPS = 1e-6
NEG_BIG = -1e30
F_MIN = 1e-30

LANES = 128
SUBLANES = 8
VMEM_LIMIT_BYTES = 56 * 1024 * 1024

CHUNK = 64
CONV_PAD = SUBLANES


def _cparams(*sem):
    return pltpu.CompilerParams(dimension_semantics=sem, vmem_limit_bytes=VMEM_LIMIT_BYTES)


def _sigmoid(x):
    return 1.0 / (1.0 + jnp.exp(-x))


def _layer_norm(y, g, b):
    mu = jnp.mean(y, axis=-1, keepdims=True)
    yc = y - mu
    var = jnp.mean(yc * yc, axis=-1, keepdims=True)
    return yc * lax.rsqrt(var + LN_EPS) * g + b


def _proj_kernel(x_ref, w_ref, o_ref):
    o_ref[...] = jnp.dot(x_ref[...].astype(BF16), w_ref[...], preferred_element_type=F32)


def _proj(x, w, tm, tn):
    T, D = x.shape
    N = w.shape[1]
    return pl.pallas_call(
        _proj_kernel,
        grid=(T // tm, N // tn),
        in_specs=[pl.BlockSpec((tm, D), lambda i, j: (i, 0)),
                  pl.BlockSpec((D, tn), lambda i, j: (0, j))],
        out_specs=pl.BlockSpec((tm, tn), lambda i, j: (i, j)),
        out_shape=jax.ShapeDtypeStruct((T, N), F32),
        compiler_params=_cparams("parallel", "arbitrary"),
        name="in_proj",
    )(x, w)


_HG_LEVELS = (32, 16, 8, 4, 2, 1)
_HG_DIAG = len(_HG_LEVELS)


def _hgrn2_tables():
    t = np.arange(CHUNK)[:, None]
    s = np.arange(CHUNK)[None, :]
    level = np.full((CHUNK, CHUNK), -1, np.int32)
    level[t == s] = _HG_DIAG
    tril = (s <= t).astype(np.float32)
    mats = [tril]
    for li, m in enumerate(_HG_LEVELS):
        mask = (t // (2 * m) == s // (2 * m)) & (t % (2 * m) >= m) & (s % (2 * m) < m)
        assert not (level[mask] >= 0).any()
        level[mask] = li
        if m < SUBLANES:
            mid = (t // (2 * m)) * (2 * m) + m
            mats.append(tril - (s <= mid - 1).astype(np.float32))
    assert (level[s < t] >= 0).all() and (level[s > t] == -1).all()
    return np.concatenate(mats, axis=0), level


def _hgrn2_kernel(q_ref, f_ref, i_ref, g_ref, lb_ref, ng_ref, w_ref, lv_ref, o_ref, st_ref, *, n_chunks):
    @pl.when(pl.program_id(2) == 0)
    def _():
        st_ref[...] = jnp.zeros_like(st_ref)

    lb = lb_ref[...]
    one_m_lb = 1.0 - lb
    ng = ng_ref[...]
    w = w_ref[...]
    lv = lv_ref[...]
    ones_bf = jnp.ones((HG_D, CHUNK), BF16)
    nt = (((1,), (1,)), ((), ()))
    st = st_ref[...]

    chunks = range(n_chunks)
    rows_of = [slice(c * CHUNK, (c + 1) * CHUNK) for c in chunks]

    q, k, v_bf, wb = [], [], [], []
    for c in chunks:
        qp = q_ref[rows_of[c], :]
        fp = f_ref[rows_of[c], :]
        q.append(qp * _sigmoid(qp))
        e_f = jnp.exp(-fp)
        sig_f = 1.0 / (1.0 + e_f)
        log_f = jnp.log(jnp.maximum(lb + one_m_lb * sig_f, F_MIN))
        k.append(one_m_lb * (e_f * sig_f))
        v_bf.append(i_ref[rows_of[c], :].astype(BF16))
        lf_hi = log_f.astype(BF16)
        lf_lo = (log_f - lf_hi.astype(F32)).astype(BF16)
        wb.append(jnp.dot(w, lf_hi, preferred_element_type=F32)
                  + jnp.dot(w, lf_lo, preferred_element_type=F32))

    p, qd, upd, dec = [], [], [], []
    for c in chunks:
        b = wb[c][0:CHUNK, :]
        b_last = b[CHUNK - 1:CHUNK, :]
        qd.append((q[c] * jnp.exp(b)).astype(BF16))
        kd = k[c] * jnp.exp(b_last - b)
        upd.append(jnp.dot(i_ref[rows_of[c], :].T.astype(BF16), kd.astype(BF16),
                           preferred_element_type=F32))
        dec.append(jnp.exp(b_last))
        pc = jnp.where(lv == _HG_DIAG,
                       jnp.dot((q[c] * k[c]).astype(BF16), ones_bf, preferred_element_type=F32), 0.0)
        small = 0
        for li, m in enumerate(_HG_LEVELS):
            if m >= SUBLANES:
                zero = jnp.zeros((m, HG_D), F32)
                qs, ks = [], []
                for lo in range(0, CHUNK, 2 * m):
                    mid = lo + m
                    ref = b[mid - 1:mid, :]
                    ks += [k[c][lo:mid, :] * jnp.exp(ref - b[lo:mid, :]), zero]
                    qs += [zero, q[c][mid:mid + m, :] * jnp.exp(b[mid:mid + m, :] - ref)]
                qt = jnp.concatenate(qs, axis=0)
                kt = jnp.concatenate(ks, axis=0)
            else:
                small += 1
                d = wb[c][small * CHUNK:(small + 1) * CHUNK, :]
                qt = q[c] * jnp.exp(jnp.minimum(d, 0.0))
                kt = k[c] * jnp.exp(jnp.minimum(-d, 0.0))
            s_l = lax.dot_general(qt.astype(BF16), kt.astype(BF16), nt, preferred_element_type=F32)
            pc = jnp.where(lv == li, s_l, pc)
        p.append(pc.astype(BF16))

    o = []
    for c in chunks:
        o.append(lax.dot_general(qd[c], st.astype(BF16), nt, preferred_element_type=F32))
        st = st * dec[c] + upd[c]

    for c in chunks:
        oc = o[c] + jnp.dot(p[c], v_bf[c], preferred_element_type=F32)
        gp = g_ref[rows_of[c], :]
        on = oc * lax.rsqrt(jnp.mean(oc * oc, axis=-1, keepdims=True) + RMS_EPS) * ng
        o_ref[rows_of[c], :] = (on * (gp * _sigmoid(gp))).astype(o_ref.dtype)

    st_ref[...] = st


def _hgrn2(proj, lb, ng, B, S, rows):
    T = B * S
    ns = S // rows
    n_chunks = rows // CHUNK
    H = HG_HEADS
    w_np, level_np = _hgrn2_tables()
    w = jnp.asarray(w_np, BF16)
    level = jnp.asarray(level_np, jnp.int32)

    def col_spec(group):
        return pl.BlockSpec((rows, HG_D), lambda b, h, j, g=group: (b * ns + j, g * H + h))

    vec_spec = pl.BlockSpec((1, HG_D), lambda b, h, j: (0, h))
    const = lambda b, h, j: (0, 0)
    return pl.pallas_call(
        functools.partial(_hgrn2_kernel, n_chunks=n_chunks),
        grid=(B, H, ns),
        in_specs=[col_spec(0), col_spec(1), col_spec(2), col_spec(3), vec_spec, vec_spec,
                  pl.BlockSpec(w.shape, const), pl.BlockSpec(level.shape, const)],
        out_specs=pl.BlockSpec((rows, HG_D), lambda b, h, j: (b * ns + j, h)),
        out_shape=jax.ShapeDtypeStruct((T, H * HG_D), BF16),
        scratch_shapes=[pltpu.VMEM((HG_D, HG_D), F32)],
        compiler_params=_cparams("parallel", "parallel", "arbitrary"),
        name="hgrn2",
    )(proj, proj, proj, proj, lb, ng, w, level)


def _mlstm_kernel(q_ref, k_ref, v_ref, og_ref, gt_ref, cw_ref, cb_ref, gb_ref, ng_ref, o_ref,
                  cq_ref, ck_ref, c_ref, n_ref, m_ref, *, n_chunks):
    rows = n_chunks * CHUNK
    W = ML_HEADS * ML_DQK

    @pl.when(pl.program_id(1) == 0)
    def _():
        cq_ref[0:CONV_PAD, :] = jnp.zeros((CONV_PAD, W), F32)
        ck_ref[0:CONV_PAD, :] = jnp.zeros((CONV_PAD, W), F32)
        c_ref[...] = jnp.zeros_like(c_ref)
        n_ref[...] = jnp.zeros_like(n_ref)
        m_ref[...] = jnp.zeros_like(m_ref)

    cq_ref[CONV_PAD:CONV_PAD + rows, :] = q_ref[...]
    ck_ref[CONV_PAD:CONV_PAD + rows, :] = k_ref[...]

    row = lax.broadcasted_iota(jnp.int32, (CHUNK, CHUNK), 0)
    col = lax.broadcasted_iota(jnp.int32, (CHUNK, CHUNK), 1)
    causal = col <= row
    tril = causal.astype(F32)
    lane = lax.broadcasted_iota(jnp.int32, (CHUNK, LANES), 1)
    k_scale = ML_DQK ** -0.5

    for c in range(n_chunks):
        r0 = c * CHUNK
        pre = gt_ref[r0:r0 + CHUNK, :] + gb_ref[...]
        log_f = jnp.minimum(pre, 0.0) - jnp.log1p(jnp.exp(-jnp.abs(pre)))
        bcum = jnp.dot(tril, log_f, precision=HIGHEST, preferred_element_type=F32)
        x = jnp.where(lane < ML_HEADS, pre, bcum)
        xt = x.T

        for h in range(ML_HEADS):
            cs = slice(h * ML_DQK, (h + 1) * ML_DQK)
            vs = slice(h * ML_DV, (h + 1) * ML_DV)

            def conv(src_ref, w_off):
                acc = cb_ref[0:1, w_off + h * ML_DQK:w_off + (h + 1) * ML_DQK]
                for j in range(CONV_W):
                    start = CONV_PAD + r0 - (CONV_W - 1) + j
                    acc = acc + (cw_ref[j:j + 1, w_off + h * ML_DQK:w_off + (h + 1) * ML_DQK]
                                 * src_ref[start:start + CHUNK, cs])
                return acc * _sigmoid(acc)

            q = conv(cq_ref, 0)
            k = conv(ck_ref, W) * k_scale
            v = v_ref[r0:r0 + CHUNK, vs]
            q_bf = q.astype(BF16)
            v_bf = v.astype(BF16)

            li_c = x[:, h:h + 1]
            b_c = x[:, ML_HEADS + h:ML_HEADS + h + 1]
            li_r = xt[h:h + 1, :]
            b_r = xt[ML_HEADS + h:ML_HEADS + h + 1, :]
            b_last = b_c[CHUNK - 1:CHUNK, :]
            m_prev = m_ref[h][0:1, 0:1]
            c_state = c_ref[h]
            n_state = n_ref[h][0:1, :]

            log_intra = jnp.where(causal, b_c - b_r + li_r, NEG_BIG)
            log_inter = b_c + m_prev
            m_t = jnp.maximum(log_inter, jnp.max(log_intra, axis=-1, keepdims=True))
            w_intra = jnp.exp(log_intra - m_t)
            w_inter = jnp.exp(log_inter - m_t)
            scores = lax.dot_general(q_bf, k.astype(BF16), (((1,), (1,)), ((), ())),
                                     preferred_element_type=F32) * w_intra
            num = (jnp.dot(scores.astype(BF16), v_bf, preferred_element_type=F32)
                   + w_inter * jnp.dot(q_bf, c_state.astype(BF16), preferred_element_type=F32))
            den = (jnp.sum(scores, axis=-1, keepdims=True)
                   + w_inter * jnp.sum(q * n_state, axis=-1, keepdims=True))
            hh = num / jnp.maximum(jnp.abs(den), jnp.exp(-m_t))

            m_new = m_t[CHUNK - 1:CHUNK, :]
            w_write = jnp.exp(b_last - b_c + li_c - m_new)
            w_carry = jnp.exp(b_last + m_prev - m_new)
            kw = k * w_write
            c_ref[h] = w_carry * c_state + jnp.dot(kw.T.astype(BF16), v_bf,
                                                   preferred_element_type=F32)
            n_new = w_carry * n_state + jnp.sum(kw, axis=0, keepdims=True)
            n_ref[h] = jnp.broadcast_to(n_new, (SUBLANES, ML_DQK))
            m_ref[h] = jnp.broadcast_to(m_new, (SUBLANES, LANES))

            hn = hh * lax.rsqrt(jnp.mean(hh * hh, axis=-1, keepdims=True) + RMS_EPS) * ng_ref[0:1, vs]
            og = og_ref[r0:r0 + CHUNK, vs]
            o_ref[r0:r0 + CHUNK, vs] = (_sigmoid(og) * hn).astype(o_ref.dtype)

    cq_ref[0:CONV_PAD, :] = cq_ref[rows:rows + CONV_PAD, :]
    ck_ref[0:CONV_PAD, :] = ck_ref[rows:rows + CONV_PAD, :]


def _mlstm(proj, gates, conv_w, conv_b, gate_bias, ng, B, S, rows, col0):
    T = B * S
    ns = S // rows
    n_chunks = rows // CHUNK
    W = ML_HEADS * ML_DQK
    V = ML_HEADS * ML_DV
    qb = col0 // W
    vb = (col0 + 2 * W) // V

    def rowmap(cb):
        return lambda b, j, cb=cb: (b * ns + j, cb)

    const = lambda b, j: (0, 0)
    return pl.pallas_call(
        functools.partial(_mlstm_kernel, n_chunks=n_chunks),
        grid=(B, ns),
        in_specs=[pl.BlockSpec((rows, W), rowmap(qb)),
                  pl.BlockSpec((rows, W), rowmap(qb + 1)),
                  pl.BlockSpec((rows, V), rowmap(vb)),
                  pl.BlockSpec((rows, V), rowmap(vb + 1)),
                  pl.BlockSpec((rows, LANES), rowmap(0)),
                  pl.BlockSpec((CONV_W, 2 * W), const),
                  pl.BlockSpec((1, 2 * W), const),
                  pl.BlockSpec((1, LANES), const),
                  pl.BlockSpec((1, V), const)],
        out_specs=pl.BlockSpec((rows, V), rowmap(0)),
        out_shape=jax.ShapeDtypeStruct((T, V), BF16),
        scratch_shapes=[pltpu.VMEM((rows + 2 * CONV_PAD, W), F32),
                        pltpu.VMEM((rows + 2 * CONV_PAD, W), F32),
                        pltpu.VMEM((ML_HEADS, ML_DQK, ML_DV), F32),
                        pltpu.VMEM((ML_HEADS, SUBLANES, ML_DQK), F32),
                        pltpu.VMEM((ML_HEADS, SUBLANES, LANES), F32)],
        compiler_params=_cparams("parallel", "arbitrary"),
        name="mlstm",
    )(proj, proj, proj, proj, gates, conv_w, conv_b, gate_bias, ng)


def _top2(logits):
    lane = lax.broadcasted_iota(jnp.int32, logits.shape, 1)
    lg = jnp.where(lane < N_EXPERTS, logits, -jnp.inf)
    m0 = jnp.max(lg, axis=-1, keepdims=True)
    i0 = jnp.min(jnp.where(lg == m0, lane, LANES), axis=-1, keepdims=True)
    lg1 = jnp.where(lane == i0, -jnp.inf, lg)
    m1 = jnp.max(lg1, axis=-1, keepdims=True)
    i1 = jnp.min(jnp.where(lg1 == m1, lane, LANES), axis=-1, keepdims=True)
    e1 = jnp.exp(m1 - m0)
    g0 = 1.0 / (1.0 + e1)
    g1 = e1 / (1.0 + e1)
    idx = jnp.where(lane == 0, i0, jnp.where(lane == 1, i1, 0))
    gate = jnp.where(lane == 0, g0, jnp.where(lane == 1, g1, 0.0))
    return idx, gate


def _outproj_kernel(hg_ref, ml_ref, w_ref, x_ref, g_ref, b_ref, *rest, route):
    if route:
        wr_ref, o_ref, idx_ref, gate_ref = rest
    else:
        (o_ref,) = rest
    half = hg_ref.shape[1]
    mixed = (jnp.dot(hg_ref[...], w_ref[0:half, :], preferred_element_type=F32)
             + jnp.dot(ml_ref[...], w_ref[half:, :], preferred_element_type=F32))
    y = _layer_norm(ALPHA * x_ref[...] + mixed, g_ref[...], b_ref[...])
    o_ref[...] = y
    if route:
        y_hi = y.astype(BF16)
        y_lo = (y - y_hi.astype(F32)).astype(BF16)
        r_hi = jnp.dot(y_hi, wr_ref[...], preferred_element_type=F32)
        r_lo = jnp.dot(y_lo, wr_ref[:, 0:LANES], preferred_element_type=F32)
        logits = r_hi[:, 0:LANES] + r_hi[:, LANES:] + r_lo
        idx, gate = _top2(logits)
        idx_ref[...] = idx
        gate_ref[...] = gate


def _outproj(hg, ml, w, x, g, b, w_router, tm):
    T, D = x.shape
    half = hg.shape[1]
    route = w_router is not None
    row = lambda i: (i, 0)
    const = lambda i: (0, 0)
    in_specs = [pl.BlockSpec((tm, half), row), pl.BlockSpec((tm, half), row),
                pl.BlockSpec((D, D), const), pl.BlockSpec((tm, D), row),
                pl.BlockSpec((1, D), const), pl.BlockSpec((1, D), const)]
    out_specs = [pl.BlockSpec((tm, D), row)]
    out_shape = [jax.ShapeDtypeStruct((T, D), F32)]
    args = [hg, ml, w, x, g, b]
    if route:
        in_specs.append(pl.BlockSpec((D, 2 * LANES), const))
        out_specs += [pl.BlockSpec((tm, LANES), row), pl.BlockSpec((tm, LANES), row)]
        out_shape += [jax.ShapeDtypeStruct((T, LANES), jnp.int32),
                      jax.ShapeDtypeStruct((T, LANES), F32)]
        args.append(w_router)
    return pl.pallas_call(
        functools.partial(_outproj_kernel, route=route),
        grid=(T // tm,),
        in_specs=in_specs, out_specs=out_specs, out_shape=out_shape,
        compiler_params=_cparams("parallel"),
        name="out_proj_route" if route else "out_proj",
    )(*args)


def _ffn_kernel(x_ref, wg_ref, wu_ref, wd_ref, g_ref, b_ref, o_ref, xb_ref, acc_ref):
    f = pl.program_id(1)

    @pl.when(f == 0)
    def _():
        xb_ref[...] = x_ref[...].astype(BF16)
        acc_ref[...] = jnp.zeros_like(acc_ref)

    xb = xb_ref[...]
    gt = jnp.dot(xb, wg_ref[...], preferred_element_type=F32)
    up = jnp.dot(xb, wu_ref[...], preferred_element_type=F32)
    h = (gt * _sigmoid(gt) * up).astype(BF16)
    acc_ref[...] += jnp.dot(h, wd_ref[...], preferred_element_type=F32)

    @pl.when(f == pl.num_programs(1) - 1)
    def _():
        o_ref[...] = _layer_norm(ALPHA * x_ref[...] + acc_ref[...], g_ref[...], b_ref[...])


def _ffn(x, wg, wu, wd, g, b, tm, tf):
    T, D = x.shape
    Fp = wg.shape[1]
    return pl.pallas_call(
        _ffn_kernel,
        grid=(T // tm, Fp // tf),
        in_specs=[pl.BlockSpec((tm, D), lambda i, f: (i, 0)),
                  pl.BlockSpec((D, tf), lambda i, f: (0, f)),
                  pl.BlockSpec((D, tf), lambda i, f: (0, f)),
                  pl.BlockSpec((tf, D), lambda i, f: (f, 0)),
                  pl.BlockSpec((1, D), lambda i, f: (0, 0)),
                  pl.BlockSpec((1, D), lambda i, f: (0, 0))],
        out_specs=pl.BlockSpec((tm, D), lambda i, f: (i, 0)),
        out_shape=jax.ShapeDtypeStruct((T, D), F32),
        scratch_shapes=[pltpu.VMEM((tm, D), BF16), pltpu.VMEM((tm, D), F32)],
        compiler_params=_cparams("parallel", "arbitrary"),
        name="dense_ffn",
    )(x, wg, wu, wd, g, b)


def _gather_rows(src_hbm, idx_ref, dst_ref, sem, n, *, wait):
    def body(r, c):
        cp = pltpu.make_async_copy(src_hbm.at[pl.ds(idx_ref[0, 0, r], 1), :],
                                   dst_ref.at[pl.ds(r, 1), :], sem)
        if wait:
            cp.wait()
        else:
            cp.start()
        return c
    lax.fori_loop(0, n, body, 0, unroll=8)


def _moe_kernel(be_ref, nv_ref, tok_ref, tok_next_ref, x_hbm, wg_ref, wu_ref, wd_ref, o_ref,
                xg_ref, xb_ref, acc_ref, sem, *, blk):
    del be_ref
    i = pl.program_id(0)
    f = pl.program_id(1)
    n_valid = nv_ref[0]
    live = i < n_valid
    slot = i % 2

    @pl.when((i == 0) & (f == 0))
    def _():
        _gather_rows(x_hbm, tok_ref, xg_ref.at[0], sem.at[0], blk, wait=False)

    @pl.when(live & (f == 0))
    def _():
        _gather_rows(x_hbm, tok_ref, xg_ref.at[slot], sem.at[slot], blk, wait=True)
        xb_ref[...] = xg_ref[slot].astype(BF16)
        acc_ref[...] = jnp.zeros_like(acc_ref)

    @pl.when((i + 1 < n_valid) & (f == 0))
    def _():
        _gather_rows(x_hbm, tok_next_ref, xg_ref.at[1 - slot], sem.at[1 - slot], blk, wait=False)

    @pl.when(live)
    def _():
        xb = xb_ref[...]
        gt = jnp.dot(xb, wg_ref[...], preferred_element_type=F32)
        up = jnp.dot(xb, wu_ref[...], preferred_element_type=F32)
        h = (gt * _sigmoid(gt) * up).astype(BF16)
        acc_ref[...] += jnp.dot(h, wd_ref[...], preferred_element_type=F32)

    @pl.when(f == pl.num_programs(1) - 1)
    def _():
        o_ref[...] = jnp.where(live, acc_ref[...], 0.0)


def _moe_ffn(block_expert, n_valid, slot_token, x, wg, wu, wd, blk, tf):
    T, D = x.shape
    F = wg.shape[2]
    n_blocks = block_expert.shape[0]
    nf = F // tf
    tok3 = slot_token.reshape(n_blocks, 1, blk)

    def f_eff(i, f, nv):
        return jnp.where(i < nv[0], f, nf - 1)

    grid_spec = pltpu.PrefetchScalarGridSpec(
        num_scalar_prefetch=2,
        grid=(n_blocks, nf),
        in_specs=[pl.BlockSpec((1, 1, blk), lambda i, f, be, nv: (i, 0, 0), memory_space=pltpu.SMEM),
                  pl.BlockSpec((1, 1, blk), lambda i, f, be, nv: (jnp.minimum(i + 1, n_blocks - 1), 0, 0),
                               memory_space=pltpu.SMEM),
                  pl.BlockSpec(memory_space=pl.ANY),
                  pl.BlockSpec((None, D, tf), lambda i, f, be, nv: (be[i], 0, f_eff(i, f, nv))),
                  pl.BlockSpec((None, D, tf), lambda i, f, be, nv: (be[i], 0, f_eff(i, f, nv))),
                  pl.BlockSpec((None, tf, D), lambda i, f, be, nv: (be[i], f_eff(i, f, nv), 0))],
        out_specs=pl.BlockSpec((blk, D), lambda i, f, be, nv: (i, 0)),
        scratch_shapes=[pltpu.VMEM((2, blk, D), F32), pltpu.VMEM((blk, D), BF16),
                        pltpu.VMEM((blk, D), F32), pltpu.SemaphoreType.DMA((2,))],
    )
    return pl.pallas_call(
        functools.partial(_moe_kernel, blk=blk),
        grid_spec=grid_spec,
        out_shape=jax.ShapeDtypeStruct((n_blocks * blk, D), F32),
        compiler_params=_cparams("arbitrary", "arbitrary"),
        name="moe_ffn",
    )(block_expert, n_valid, tok3, tok3, x, wg, wu, wd)


def _combine_kernel(dest_ref, dest_next_ref, ys_hbm, x_ref, gate_ref, g_ref, b_ref, o_ref,
                    buf_ref, sem, *, tm):
    i = pl.program_id(0)
    slot = i % 2
    n = TOP_K * tm

    @pl.when(i == 0)
    def _():
        _gather_rows(ys_hbm, dest_ref, buf_ref.at[0], sem.at[0], n, wait=False)

    @pl.when(i + 1 < pl.num_programs(0))
    def _():
        _gather_rows(ys_hbm, dest_next_ref, buf_ref.at[1 - slot], sem.at[1 - slot], n, wait=False)

    _gather_rows(ys_hbm, dest_ref, buf_ref.at[slot], sem.at[slot], n, wait=True)
    gate = gate_ref[...]
    ff = buf_ref[slot, 0:tm, :] * gate[:, 0:1] + buf_ref[slot, tm:2 * tm, :] * gate[:, 1:2]
    o_ref[...] = _layer_norm(ALPHA * x_ref[...] + ff, g_ref[...], b_ref[...])


def _combine(dest, ys, x, gate, g, b, tm):
    T, D = x.shape
    nt = T // tm
    dest3 = dest.reshape(nt, tm, TOP_K).transpose(0, 2, 1).reshape(nt, 1, TOP_K * tm)
    return pl.pallas_call(
        functools.partial(_combine_kernel, tm=tm),
        grid=(nt,),
        in_specs=[pl.BlockSpec((1, 1, TOP_K * tm), lambda i: (i, 0, 0), memory_space=pltpu.SMEM),
                  pl.BlockSpec((1, 1, TOP_K * tm), lambda i: (jnp.minimum(i + 1, nt - 1), 0, 0),
                               memory_space=pltpu.SMEM),
                  pl.BlockSpec(memory_space=pl.ANY),
                  pl.BlockSpec((tm, D), lambda i: (i, 0)),
                  pl.BlockSpec((tm, LANES), lambda i: (i, 0)),
                  pl.BlockSpec((1, D), lambda i: (0, 0)),
                  pl.BlockSpec((1, D), lambda i: (0, 0))],
        out_specs=pl.BlockSpec((tm, D), lambda i: (i, 0)),
        out_shape=jax.ShapeDtypeStruct((T, D), F32),
        scratch_shapes=[pltpu.VMEM((2, TOP_K * tm, D), F32), pltpu.SemaphoreType.DMA((2,))],
        compiler_params=_cparams("arbitrary"),
        name="moe_combine",
    )(dest3, dest3, ys, x, gate, g, b)


def _dispatch(idx2, blk, n_blocks):
    T = idx2.shape[0]
    n_assign = T * TOP_K
    expert = idx2.reshape(-1)
    onehot = (expert[:, None] == jnp.arange(N_EXPERTS, dtype=jnp.int32)[None, :]).astype(jnp.int32)
    csum = jnp.cumsum(onehot, axis=0)
    rank = jnp.sum(onehot * (csum - 1), axis=1)
    counts = csum[-1]
    padded = (counts + blk - 1) // blk * blk
    padded_end = jnp.cumsum(padded)
    dest = (padded_end - padded)[expert] + rank
    token = jnp.arange(n_assign, dtype=jnp.int32) // TOP_K
    slot_token = jnp.zeros((n_blocks * blk,), jnp.int32).at[dest].set(token)
    block_expert = jnp.minimum(
        jnp.searchsorted(padded_end, jnp.arange(n_blocks, dtype=jnp.int32) * blk, side="right"),
        N_EXPERTS - 1).astype(jnp.int32)
    n_valid = (padded_end[-1:] // blk).astype(jnp.int32)
    block_ids = jnp.arange(n_blocks, dtype=jnp.int32)
    block_expert = jnp.where(block_ids < n_valid[0], block_expert,
                             block_expert[jnp.maximum(n_valid[0] - 1, 0)])
    return dest.reshape(T, TOP_K).astype(jnp.int32), slot_token, block_expert, n_valid


def _tile(n, pref):
    t = min(n, pref)
    while n % t:
        t //= 2
    return t


def kernel(x, w_in, w_out, hg_lb_logits, hg_norm_g, ml_conv_w, ml_conv_b, ml_b_i, ml_b_f, ml_norm_g,
           ln_mix_g, ln_mix_b, ln_ffn_g, ln_ffn_b, ffn_w_gate, ffn_w_up, ffn_w_down,
           moe_w_router, moe_w_gate, moe_w_up, moe_w_down):
    B, S, D = x.shape
    T = B * S
    depth = w_in.shape[0]
    hg_w = HG_HEADS * HG_D
    main_cols = 4 * hg_w + 2 * ML_HEADS * ML_DQK + 2 * ML_HEADS * ML_DV

    p = jax.nn.softmax(hg_lb_logits.astype(F32), axis=0)
    lower_bounds = jnp.cumsum(p, axis=0) - p[0]

    rows = _tile(S, 256)
    tm_proj = _tile(T, 1024)
    tm = _tile(T, 512)
    blk = _tile(T, 512)
    n_blocks = -(-(T * TOP_K + N_EXPERTS * (blk - 1)) // blk)

    f_dense = ffn_w_gate.shape[2]
    tf_dense = 512
    f_pad = -(-f_dense // tf_dense) * tf_dense - f_dense

    xt = x.reshape(T, D).astype(F32)
    for layer in range(depth):
        w_main = w_in[layer, :, :main_cols].astype(BF16)
        w_gates = jnp.pad(w_in[layer, :, main_cols:], ((0, 0), (0, LANES - 2 * ML_HEADS))).astype(BF16)
        proj = _proj(xt, w_main, tm_proj, 1024)
        gates = _proj(xt, w_gates, tm_proj, LANES)

        hg = _hgrn2(proj, lower_bounds[layer][None, :], hg_norm_g[layer][None, :].astype(F32), B, S, rows)
        gate_bias = jnp.pad(jnp.concatenate([ml_b_i[layer], ml_b_f[layer]]).astype(F32),
                            (0, LANES - 2 * ML_HEADS))[None, :]
        ml = _mlstm(proj, gates, ml_conv_w[layer].astype(F32), ml_conv_b[layer][None, :].astype(F32),
                    gate_bias, ml_norm_g[layer][None, :].astype(F32), B, S, rows, 4 * hg_w)

        moe = layer % 2 == 1
        j = layer // 2
        w_router = None
        if moe:
            wr = jnp.pad(moe_w_router[j].astype(F32), ((0, 0), (0, LANES - N_EXPERTS)))
            wr_hi = wr.astype(BF16)
            w_router = jnp.concatenate([wr_hi, (wr - wr_hi.astype(F32)).astype(BF16)], axis=1)
        res = _outproj(hg, ml, w_out[layer].astype(BF16), xt,
                       ln_mix_g[layer][None, :], ln_mix_b[layer][None, :], w_router, tm)
        g2 = ln_ffn_g[layer][None, :]
        b2 = ln_ffn_b[layer][None, :]
        if not moe:
            (x1,) = res
            wg = jnp.pad(ffn_w_gate[j].astype(BF16), ((0, 0), (0, f_pad)))
            wu = jnp.pad(ffn_w_up[j].astype(BF16), ((0, 0), (0, f_pad)))
            wd = jnp.pad(ffn_w_down[j].astype(BF16), ((0, f_pad), (0, 0)))
            xt = _ffn(x1, wg, wu, wd, g2, b2, tm, tf_dense)
        else:
            x1, idx, gate = res
            dest, slot_token, block_expert, n_valid = _dispatch(idx[:, :TOP_K], blk, n_blocks)
            ys = _moe_ffn(block_expert, n_valid, slot_token, x1, moe_w_gate[j].astype(BF16),
                          moe_w_up[j].astype(BF16), moe_w_down[j].astype(BF16), blk, 1024)
            xt = _combine(dest, ys, x1, gate, g2, b2, _tile(T, 256))
    return xt.reshape(B, S, D).astype(x.dtype)
```

```python
import functools

import jax
import jax.numpy as jnp
import numpy as np
from jax import lax
from jax.experimental import pallas as pl
from jax.experimental.pallas import tpu as pltpu

F32 = jnp.float32
BF16 = jnp.bfloat16
HIGHEST = lax.Precision.HIGHEST

MODEL_DEPTH = 4
HG_HEADS = 8
HG_D = 128
ML_HEADS = 4
ML_DQK = 128
ML_DV = 256
CONV_W = 4
N_EXPERTS = 8
TOP_K = 2
ALPHA = (2 * MODEL_DEPTH) ** 0.25
LN_EPS = 1e-5
RMS_EPS = 1e-6
NEG_BIG = -1e30
F_MIN = 1e-30

LANES = 128
SUBLANES = 8
VMEM_LIMIT_BYTES = 56 * 1024 * 1024

CHUNK = 64
CONV_PAD = SUBLANES


def _cparams(*sem):
    return pltpu.CompilerParams(dimension_semantics=sem, vmem_limit_bytes=VMEM_LIMIT_BYTES)


def _sigmoid(x):
    return 1.0 / (1.0 + jnp.exp(-x))


def _layer_norm(y, g, b):
    mu = jnp.mean(y, axis=-1, keepdims=True)
    yc = y - mu
    var = jnp.mean(yc * yc, axis=-1, keepdims=True)
    return yc * lax.rsqrt(var + LN_EPS) * g + b


def _cast_kernel(x_ref, o_ref, *, rows_in, cols_in, mask):
    x = x_ref[...]
    if mask:
        tr, tc = x.shape
        r = pl.program_id(0) * tr + lax.broadcasted_iota(jnp.int32, x.shape, 0)
        c = pl.program_id(1) * tc + lax.broadcasted_iota(jnp.int32, x.shape, 1)
        x = jnp.where((r < rows_in) & (c < cols_in), x, 0.0)
    o_ref[...] = x.astype(BF16)


def _cast_bf16(w, layer, rows_out, cols_out, tr, tc):
    _, R, C = w.shape
    mask = rows_out > R or cols_out > C
    return pl.pallas_call(
        functools.partial(_cast_kernel, rows_in=R, cols_in=C, mask=mask),
        grid=(rows_out // tr, cols_out // tc),
        in_specs=[pl.BlockSpec((None, tr, tc), lambda i, j: (layer, i, j))],
        out_specs=pl.BlockSpec((tr, tc), lambda i, j: (i, j)),
        out_shape=jax.ShapeDtypeStruct((rows_out, cols_out), BF16),
        compiler_params=_cparams("parallel", "parallel"),
        name="cast_bf16",
    )(w)


def _proj_kernel(x_ref, w_ref, wgate_ref, o_ref, ogate_ref):
    xb = x_ref[...].astype(BF16)
    o_ref[...] = jnp.dot(xb, w_ref[...], preferred_element_type=F32)

    @pl.when(pl.program_id(1) == 0)
    def _():
        ogate_ref[...] = jnp.dot(xb, wgate_ref[...], preferred_element_type=F32)


def _proj(x, w, w_gates, tm, tn):
    T, D = x.shape
    N = w.shape[1]
    return pl.pallas_call(
        _proj_kernel,
        grid=(T // tm, N // tn),
        in_specs=[pl.BlockSpec((tm, D), lambda i, j: (i, 0)),
                  pl.BlockSpec((D, tn), lambda i, j: (0, j)),
                  pl.BlockSpec((D, LANES), lambda i, j: (0, 0))],
        out_specs=[pl.BlockSpec((tm, tn), lambda i, j: (i, j)),
                   pl.BlockSpec((tm, LANES), lambda i, j: (i, 0))],
        out_shape=[jax.ShapeDtypeStruct((T, N), F32), jax.ShapeDtypeStruct((T, LANES), F32)],
        compiler_params=_cparams("parallel", "arbitrary"),
        name="in_proj",
    )(x, w, w_gates)


_HG_LEVELS = (32, 16, 8, 4, 2, 1)
_HG_DIAG = len(_HG_LEVELS)


def _hgrn2_tables():
    t = np.arange(CHUNK)[:, None]
    s = np.arange(CHUNK)[None, :]
    level = np.full((CHUNK, CHUNK), -1, np.int32)
    level[t == s] = _HG_DIAG
    tril = (s <= t).astype(np.float32)
    mats = [tril]
    for li, m in enumerate(_HG_LEVELS):
        mask = (t // (2 * m) == s // (2 * m)) & (t % (2 * m) >= m) & (s % (2 * m) < m)
        assert not (level[mask] >= 0).any()
        level[mask] = li
        if m < SUBLANES:
            mid = (t // (2 * m)) * (2 * m) + m
            mats.append(tril - (s <= mid - 1).astype(np.float32))
    assert (level[s < t] >= 0).all() and (level[s > t] == -1).all()
    return np.concatenate(mats, axis=0), level


def _hgrn2_kernel(q_ref, f_ref, i_ref, g_ref, lb_ref, ng_ref, w_ref, lv_ref, o_ref, st_ref,
                  *, n_chunks, n_heads):
    @pl.when(pl.program_id(2) == 0)
    def _():
        st_ref[...] = jnp.zeros_like(st_ref)

    w = w_ref[...]
    lv = lv_ref[...]
    ones_bf = jnp.ones((HG_D, CHUNK), BF16)
    nt = (((1,), (1,)), ((), ()))
    units = [(hh, c) for hh in range(n_heads) for c in range(n_chunks)]
    rows_of = {u: slice(u[1] * CHUNK, (u[1] + 1) * CHUNK) for u in units}
    lanes_of = {u: slice(u[0] * HG_D, (u[0] + 1) * HG_D) for u in units}

    q, k, v_bf, wb = {}, {}, {}, {}
    for u in units:
        lb = lb_ref[:, lanes_of[u]]
        one_m_lb = 1.0 - lb
        qp = q_ref[rows_of[u], lanes_of[u]]
        fp = f_ref[rows_of[u], lanes_of[u]]
        q[u] = qp * _sigmoid(qp)
        e_f = jnp.exp(-fp)
        sig_f = 1.0 / (1.0 + e_f)
        log_f = jnp.log2(jnp.maximum(lb + one_m_lb * sig_f, F_MIN))
        k[u] = one_m_lb * (e_f * sig_f)
        v_bf[u] = i_ref[rows_of[u], lanes_of[u]].astype(BF16)
        lf_hi = log_f.astype(BF16)
        lf_lo = (log_f - lf_hi.astype(F32)).astype(BF16)
        wb[u] = (jnp.dot(w, lf_hi, preferred_element_type=F32)
                 + jnp.dot(w, lf_lo, preferred_element_type=F32))

    p, qd, upd, dec = {}, {}, {}, {}
    for u in units:
        b = wb[u][0:CHUNK, :]
        b_last = b[CHUNK - 1:CHUNK, :]
        qd[u] = (q[u] * jnp.exp2(b)).astype(BF16)
        kd = k[u] * jnp.exp2(b_last - b)
        upd[u] = jnp.dot(i_ref[rows_of[u], lanes_of[u]].T.astype(BF16), kd.astype(BF16),
                         preferred_element_type=F32)
        dec[u] = jnp.exp2(b_last)
        pc = jnp.where(lv == _HG_DIAG,
                       jnp.dot((q[u] * k[u]).astype(BF16), ones_bf, preferred_element_type=F32), 0.0)
        small = 0
        for li, m in enumerate(_HG_LEVELS):
            if m >= SUBLANES:
                zero = jnp.zeros((m, HG_D), F32)
                qs, ks = [], []
                for lo in range(0, CHUNK, 2 * m):
                    mid = lo + m
                    ref = b[mid - 1:mid, :]
                    ks += [k[u][lo:mid, :] * jnp.exp2(ref - b[lo:mid, :]), zero]
                    qs += [zero, q[u][mid:mid + m, :] * jnp.exp2(b[mid:mid + m, :] - ref)]
                qt = jnp.concatenate(qs, axis=0)
                kt = jnp.concatenate(ks, axis=0)
            else:
                small += 1
                d = wb[u][small * CHUNK:(small + 1) * CHUNK, :]
                qt = q[u] * jnp.exp2(jnp.minimum(d, 0.0))
                kt = k[u] * jnp.exp2(jnp.minimum(-d, 0.0))
            s_l = lax.dot_general(qt.astype(BF16), kt.astype(BF16), nt, preferred_element_type=F32)
            pc = jnp.where(lv == li, s_l, pc)
        p[u] = pc.astype(BF16)

    o = {}
    for hh in range(n_heads):
        st = st_ref[hh]
        for c in range(n_chunks):
            u = (hh, c)
            o[u] = lax.dot_general(qd[u], st.astype(BF16), nt, preferred_element_type=F32)
            st = st * dec[u] + upd[u]
        st_ref[hh] = st

    for u in units:
        oc = o[u] + jnp.dot(p[u], v_bf[u], preferred_element_type=F32)
        gp = g_ref[rows_of[u], lanes_of[u]]
        on = oc * lax.rsqrt(jnp.mean(oc * oc, axis=-1, keepdims=True) + RMS_EPS) * ng_ref[:, lanes_of[u]]
        o_ref[rows_of[u], lanes_of[u]] = (on * (gp * _sigmoid(gp))).astype(o_ref.dtype)


def _hgrn2(proj, lb, ng, B, S, rows, heads_per_step):
    T = B * S
    ns = S // rows
    n_chunks = rows // CHUNK
    H = HG_HEADS
    hp = heads_per_step
    wd = hp * HG_D
    w_np, level_np = _hgrn2_tables()
    w = jnp.asarray(w_np, BF16)
    level = jnp.asarray(level_np, jnp.int32)

    def col_spec(group):
        return pl.BlockSpec((rows, wd), lambda b, h, j, g=group: (b * ns + j, g * (H // hp) + h))

    vec_spec = pl.BlockSpec((1, wd), lambda b, h, j: (0, h))
    const = lambda b, h, j: (0, 0)
    return pl.pallas_call(
        functools.partial(_hgrn2_kernel, n_chunks=n_chunks, n_heads=hp),
        grid=(B, H // hp, ns),
        in_specs=[col_spec(0), col_spec(1), col_spec(2), col_spec(3), vec_spec, vec_spec,
                  pl.BlockSpec(w.shape, const), pl.BlockSpec(level.shape, const)],
        out_specs=pl.BlockSpec((rows, wd), lambda b, h, j: (b * ns + j, h)),
        out_shape=jax.ShapeDtypeStruct((T, H * HG_D), BF16),
        scratch_shapes=[pltpu.VMEM((hp, HG_D, HG_D), F32)],
        compiler_params=_cparams("parallel", "parallel", "arbitrary"),
        name="hgrn2",
    )(proj, proj, proj, proj, lb, ng, w, level)


def _mlstm_kernel(q_ref, k_ref, v_ref, og_ref, gt_ref, cw_ref, cb_ref, gb_ref, ng_ref, o_ref,
                  cq_ref, ck_ref, c_ref, n_ref, m_ref, *, n_chunks):
    rows = n_chunks * CHUNK
    W = ML_HEADS * ML_DQK

    @pl.when(pl.program_id(1) == 0)
    def _():
        cq_ref[0:CONV_PAD, :] = jnp.zeros((CONV_PAD, W), F32)
        ck_ref[0:CONV_PAD, :] = jnp.zeros((CONV_PAD, W), F32)
        c_ref[...] = jnp.zeros_like(c_ref)
        n_ref[...] = jnp.zeros_like(n_ref)
        m_ref[...] = jnp.zeros_like(m_ref)

    cq_ref[CONV_PAD:CONV_PAD + rows, :] = q_ref[...]
    ck_ref[CONV_PAD:CONV_PAD + rows, :] = k_ref[...]

    row = lax.broadcasted_iota(jnp.int32, (CHUNK, CHUNK), 0)
    col = lax.broadcasted_iota(jnp.int32, (CHUNK, CHUNK), 1)
    causal = col <= row
    tril = causal.astype(F32)
    lane = lax.broadcasted_iota(jnp.int32, (CHUNK, LANES), 1)
    k_scale = ML_DQK ** -0.5

    for c in range(n_chunks):
        r0 = c * CHUNK
        pre = gt_ref[r0:r0 + CHUNK, :] + gb_ref[...]
        log_f = jnp.minimum(pre, 0.0) - jnp.log1p(jnp.exp(-jnp.abs(pre)))
        bcum = jnp.dot(tril, log_f, precision=HIGHEST, preferred_element_type=F32)
        x = jnp.where(lane < ML_HEADS, pre, bcum)
        xt = x.T

        for h in range(ML_HEADS):
            cs = slice(h * ML_DQK, (h + 1) * ML_DQK)
            vs = slice(h * ML_DV, (h + 1) * ML_DV)

            def conv(src_ref, w_off):
                acc = cb_ref[0:1, w_off + h * ML_DQK:w_off + (h + 1) * ML_DQK]
                for j in range(CONV_W):
                    start = CONV_PAD + r0 - (CONV_W - 1) + j
                    acc = acc + (cw_ref[j:j + 1, w_off + h * ML_DQK:w_off + (h + 1) * ML_DQK]
                                 * src_ref[start:start + CHUNK, cs])
                return acc * _sigmoid(acc)

            q = conv(cq_ref, 0)
            k = conv(ck_ref, W) * k_scale
            v = v_ref[r0:r0 + CHUNK, vs]
            q_bf = q.astype(BF16)
            v_bf = v.astype(BF16)

            li_c = x[:, h:h + 1]
            b_c = x[:, ML_HEADS + h:ML_HEADS + h + 1]
            li_r = xt[h:h + 1, :]
            b_r = xt[ML_HEADS + h:ML_HEADS + h + 1, :]
            b_last = b_c[CHUNK - 1:CHUNK, :]
            m_prev = m_ref[h][0:1, 0:1]
            c_state = c_ref[h]
            n_state = n_ref[h][0:1, :]

            log_intra = jnp.where(causal, b_c - b_r + li_r, NEG_BIG)
            log_inter = b_c + m_prev
            m_t = jnp.maximum(log_inter, jnp.max(log_intra, axis=-1, keepdims=True))
            w_intra = jnp.exp(log_intra - m_t)
            w_inter = jnp.exp(log_inter - m_t)
            scores = lax.dot_general(q_bf, k.astype(BF16), (((1,), (1,)), ((), ())),
                                     preferred_element_type=F32) * w_intra
            num = (jnp.dot(scores.astype(BF16), v_bf, preferred_element_type=F32)
                   + w_inter * jnp.dot(q_bf, c_state.astype(BF16), preferred_element_type=F32))
            den = (jnp.sum(scores, axis=-1, keepdims=True)
                   + w_inter * jnp.sum(q * n_state, axis=-1, keepdims=True))
            hh = num / jnp.maximum(jnp.abs(den), jnp.exp(-m_t))

            m_new = m_t[CHUNK - 1:CHUNK, :]
            w_write = jnp.exp(b_last - b_c + li_c - m_new)
            w_carry = jnp.exp(b_last + m_prev - m_new)
            kw = k * w_write
            c_ref[h] = w_carry * c_state + jnp.dot(kw.T.astype(BF16), v_bf,
                                                   preferred_element_type=F32)
            n_new = w_carry * n_state + jnp.sum(kw, axis=0, keepdims=True)
            n_ref[h] = jnp.broadcast_to(n_new, (SUBLANES, ML_DQK))
            m_ref[h] = jnp.broadcast_to(m_new, (SUBLANES, LANES))

            hn = hh * lax.rsqrt(jnp.mean(hh * hh, axis=-1, keepdims=True) + RMS_EPS) * ng_ref[0:1, vs]
            og = og_ref[r0:r0 + CHUNK, vs]
            o_ref[r0:r0 + CHUNK, vs] = (_sigmoid(og) * hn).astype(o_ref.dtype)

    cq_ref[0:CONV_PAD, :] = cq_ref[rows:rows + CONV_PAD, :]
    ck_ref[0:CONV_PAD, :] = ck_ref[rows:rows + CONV_PAD, :]


def _mlstm(proj, gates, conv_w, conv_b, gate_bias, ng, B, S, rows, col0):
    T = B * S
    ns = S // rows
    n_chunks = rows // CHUNK
    W = ML_HEADS * ML_DQK
    V = ML_HEADS * ML_DV
    qb = col0 // W
    vb = (col0 + 2 * W) // V

    def rowmap(cb):
        return lambda b, j, cb=cb: (b * ns + j, cb)

    const = lambda b, j: (0, 0)
    return pl.pallas_call(
        functools.partial(_mlstm_kernel, n_chunks=n_chunks),
        grid=(B, ns),
        in_specs=[pl.BlockSpec((rows, W), rowmap(qb)),
                  pl.BlockSpec((rows, W), rowmap(qb + 1)),
                  pl.BlockSpec((rows, V), rowmap(vb)),
                  pl.BlockSpec((rows, V), rowmap(vb + 1)),
                  pl.BlockSpec((rows, LANES), rowmap(0)),
                  pl.BlockSpec((CONV_W, 2 * W), const),
                  pl.BlockSpec((1, 2 * W), const),
                  pl.BlockSpec((1, LANES), const),
                  pl.BlockSpec((1, V), const)],
        out_specs=pl.BlockSpec((rows, V), rowmap(0)),
        out_shape=jax.ShapeDtypeStruct((T, V), BF16),
        scratch_shapes=[pltpu.VMEM((rows + 2 * CONV_PAD, W), F32),
                        pltpu.VMEM((rows + 2 * CONV_PAD, W), F32),
                        pltpu.VMEM((ML_HEADS, ML_DQK, ML_DV), F32),
                        pltpu.VMEM((ML_HEADS, SUBLANES, ML_DQK), F32),
                        pltpu.VMEM((ML_HEADS, SUBLANES, LANES), F32)],
        compiler_params=_cparams("parallel", "arbitrary"),
        name="mlstm",
    )(proj, proj, proj, proj, gates, conv_w, conv_b, gate_bias, ng)


def _top2(logits):
    lane = lax.broadcasted_iota(jnp.int32, logits.shape, 1)
    lg = jnp.where(lane < N_EXPERTS, logits, -jnp.inf)
    m0 = jnp.max(lg, axis=-1, keepdims=True)
    i0 = jnp.min(jnp.where(lg == m0, lane, LANES), axis=-1, keepdims=True)
    lg1 = jnp.where(lane == i0, -jnp.inf, lg)
    m1 = jnp.max(lg1, axis=-1, keepdims=True)
    i1 = jnp.min(jnp.where(lg1 == m1, lane, LANES), axis=-1, keepdims=True)
    e1 = jnp.exp(m1 - m0)
    g0 = 1.0 / (1.0 + e1)
    g1 = e1 / (1.0 + e1)
    idx = jnp.where(lane == 0, i0, jnp.where(lane == 1, i1, 0))
    gate = jnp.where(lane == 0, g0, jnp.where(lane == 1, g1, 0.0))
    return idx, gate


def _outproj_kernel(hg_ref, ml_ref, w_ref, x_ref, g_ref, b_ref, *rest, route):
    if route:
        wr_ref, o_ref, idx_ref, gate_ref = rest
    else:
        (o_ref,) = rest
    half = hg_ref.shape[1]
    tm = x_ref.shape[0]
    parts = [slice(0, tm // 2), slice(tm // 2, tm)]
    mixed = [jnp.dot(hg_ref[r, :], w_ref[0:half, :], preferred_element_type=F32)
             + jnp.dot(ml_ref[r, :], w_ref[half:, :], preferred_element_type=F32) for r in parts]
    y = jnp.concatenate([_layer_norm(ALPHA * x_ref[r, :] + m, g_ref[...], b_ref[...])
                         for r, m in zip(parts, mixed)], axis=0)
    o_ref[...] = y
    if route:
        y_hi = y.astype(BF16)
        y_lo = (y - y_hi.astype(F32)).astype(BF16)
        r_hi = jnp.dot(y_hi, wr_ref[...], preferred_element_type=F32)
        r_lo = jnp.dot(y_lo, wr_ref[:, 0:LANES], preferred_element_type=F32)
        logits = r_hi[:, 0:LANES] + r_hi[:, LANES:] + r_lo
        idx, gate = _top2(logits)
        idx_ref[...] = idx
        gate_ref[...] = gate


def _outproj(hg, ml, w, x, g, b, w_router, tm):
    T, D = x.shape
    half = hg.shape[1]
    route = w_router is not None
    row = lambda i: (i, 0)
    const = lambda i: (0, 0)
    in_specs = [pl.BlockSpec((tm, half), row), pl.BlockSpec((tm, half), row),
                pl.BlockSpec((D, D), const), pl.BlockSpec((tm, D), row),
                pl.BlockSpec((1, D), const), pl.BlockSpec((1, D), const)]
    out_specs = [pl.BlockSpec((tm, D), row)]
    out_shape = [jax.ShapeDtypeStruct((T, D), F32)]
    args = [hg, ml, w, x, g, b]
    if route:
        in_specs.append(pl.BlockSpec((D, 2 * LANES), const))
        out_specs += [pl.BlockSpec((tm, LANES), row), pl.BlockSpec((tm, LANES), row)]
        out_shape += [jax.ShapeDtypeStruct((T, LANES), jnp.int32),
                      jax.ShapeDtypeStruct((T, LANES), F32)]
        args.append(w_router)
    return pl.pallas_call(
        functools.partial(_outproj_kernel, route=route),
        grid=(T // tm,),
        in_specs=in_specs, out_specs=out_specs, out_shape=out_shape,
        compiler_params=_cparams("parallel"),
        name="out_proj_route" if route else "out_proj",
    )(*args)


def _ffn_kernel(x_ref, wg_ref, wu_ref, wd_ref, g_ref, b_ref, o_ref, xb_ref):
    f = pl.program_id(1)

    @pl.when(f == 0)
    def _():
        xb_ref[...] = x_ref[...].astype(BF16)
        o_ref[...] = jnp.zeros_like(o_ref)

    xb = xb_ref[...]
    gt = jnp.dot(xb, wg_ref[...], preferred_element_type=F32)
    up = jnp.dot(xb, wu_ref[...], preferred_element_type=F32)
    h = (gt * _sigmoid(gt) * up).astype(BF16)
    o_ref[...] += jnp.dot(h, wd_ref[...], preferred_element_type=F32)

    @pl.when(f == pl.num_programs(1) - 1)
    def _():
        o_ref[...] = _layer_norm(ALPHA * x_ref[...] + o_ref[...], g_ref[...], b_ref[...])


def _ffn(x, wg, wu, wd, g, b, tm, tf):
    T, D = x.shape
    Fp = wg.shape[1]
    return pl.pallas_call(
        _ffn_kernel,
        grid=(T // tm, Fp // tf),
        in_specs=[pl.BlockSpec((tm, D), lambda i, f: (i, 0)),
                  pl.BlockSpec((D, tf), lambda i, f: (0, f)),
                  pl.BlockSpec((D, tf), lambda i, f: (0, f)),
                  pl.BlockSpec((tf, D), lambda i, f: (f, 0)),
                  pl.BlockSpec((1, D), lambda i, f: (0, 0)),
                  pl.BlockSpec((1, D), lambda i, f: (0, 0))],
        out_specs=pl.BlockSpec((tm, D), lambda i, f: (i, 0)),
        out_shape=jax.ShapeDtypeStruct((T, D), F32),
        scratch_shapes=[pltpu.VMEM((tm, D), BF16)],
        compiler_params=_cparams("parallel", "arbitrary"),
        name="dense_ffn",
    )(x, wg, wu, wd, g, b)


def _gather_rows(src_hbm, idx_ref, dst_ref, sem, n, *, wait):
    def body(r, c):
        cp = pltpu.make_async_copy(src_hbm.at[pl.ds(idx_ref[0, 0, r], 1), :],
                                   dst_ref.at[pl.ds(r, 1), :], sem)
        if wait:
            cp.wait()
        else:
            cp.start()
        return c
    lax.fori_loop(0, n, body, 0, unroll=8)


def _moe_kernel(be_ref, nv_ref, tok_ref, tok_next_ref, x_hbm, wg_ref, wu_ref, wd_ref, o_ref,
                xg_ref, xb_ref, acc_ref, sem, *, blk):
    del be_ref
    i = pl.program_id(0)
    f = pl.program_id(1)
    n_valid = nv_ref[0]
    live = i < n_valid
    slot = i % 2

    @pl.when((i == 0) & (f == 0))
    def _():
        _gather_rows(x_hbm, tok_ref, xg_ref.at[0], sem.at[0], blk, wait=False)

    @pl.when(live & (f == 0))
    def _():
        _gather_rows(x_hbm, tok_ref, xg_ref.at[slot], sem.at[slot], blk, wait=True)
        xb_ref[...] = xg_ref[slot].astype(BF16)
        acc_ref[...] = jnp.zeros_like(acc_ref)

    @pl.when((i + 1 < n_valid) & (f == 0))
    def _():
        _gather_rows(x_hbm, tok_next_ref, xg_ref.at[1 - slot], sem.at[1 - slot], blk, wait=False)

    @pl.when(live)
    def _():
        xb = xb_ref[...]
        gt = jnp.dot(xb, wg_ref[...], preferred_element_type=F32)
        up = jnp.dot(xb, wu_ref[...], preferred_element_type=F32)
        h = (gt * _sigmoid(gt) * up).astype(BF16)
        acc_ref[...] += jnp.dot(h, wd_ref[...], preferred_element_type=F32)

    @pl.when(f == pl.num_programs(1) - 1)
    def _():
        o_ref[...] = jnp.where(live, acc_ref[...], 0.0)


def _moe_ffn(block_expert, n_valid, slot_token, x, wg, wu, wd, blk, tf):
    T, D = x.shape
    F = wg.shape[2]
    n_blocks = block_expert.shape[0]
    nf = F // tf
    tok3 = slot_token.reshape(n_blocks, 1, blk)

    def f_eff(i, f, nv):
        return jnp.where(i < nv[0], f, nf - 1)

    grid_spec = pltpu.PrefetchScalarGridSpec(
        num_scalar_prefetch=2,
        grid=(n_blocks, nf),
        in_specs=[pl.BlockSpec((1, 1, blk), lambda i, f, be, nv: (i, 0, 0), memory_space=pltpu.SMEM),
                  pl.BlockSpec((1, 1, blk), lambda i, f, be, nv: (jnp.minimum(i + 1, n_blocks - 1), 0, 0),
                               memory_space=pltpu.SMEM),
                  pl.BlockSpec(memory_space=pl.ANY),
                  pl.BlockSpec((None, D, tf), lambda i, f, be, nv: (be[i], 0, f_eff(i, f, nv))),
                  pl.BlockSpec((None, D, tf), lambda i, f, be, nv: (be[i], 0, f_eff(i, f, nv))),
                  pl.BlockSpec((None, tf, D), lambda i, f, be, nv: (be[i], f_eff(i, f, nv), 0))],
        out_specs=pl.BlockSpec((blk, D), lambda i, f, be, nv: (i, 0)),
        scratch_shapes=[pltpu.VMEM((2, blk, D), F32), pltpu.VMEM((blk, D), BF16),
                        pltpu.VMEM((blk, D), F32), pltpu.SemaphoreType.DMA((2,))],
    )
    return pl.pallas_call(
        functools.partial(_moe_kernel, blk=blk),
        grid_spec=grid_spec,
        out_shape=jax.ShapeDtypeStruct((n_blocks * blk, D), F32),
        compiler_params=_cparams("arbitrary", "arbitrary"),
        name="moe_ffn",
    )(block_expert, n_valid, tok3, tok3, x, wg, wu, wd)


def _combine_kernel(dest_ref, dest_next_ref, ys_hbm, x_ref, gate_ref, g_ref, b_ref, o_ref,
                    buf_ref, sem, *, tm):
    i = pl.program_id(0)
    slot = i % 2
    n = TOP_K * tm

    @pl.when(i == 0)
    def _():
        _gather_rows(ys_hbm, dest_ref, buf_ref.at[0], sem.at[0], n, wait=False)

    @pl.when(i + 1 < pl.num_programs(0))
    def _():
        _gather_rows(ys_hbm, dest_next_ref, buf_ref.at[1 - slot], sem.at[1 - slot], n, wait=False)

    _gather_rows(ys_hbm, dest_ref, buf_ref.at[slot], sem.at[slot], n, wait=True)
    gate = gate_ref[...]
    ff = buf_ref[slot, 0:tm, :] * gate[:, 0:1] + buf_ref[slot, tm:2 * tm, :] * gate[:, 1:2]
    o_ref[...] = _layer_norm(ALPHA * x_ref[...] + ff, g_ref[...], b_ref[...])


def _combine(dest, ys, x, gate, g, b, tm):
    T, D = x.shape
    nt = T // tm
    dest3 = dest.reshape(nt, tm, TOP_K).transpose(0, 2, 1).reshape(nt, 1, TOP_K * tm)
    return pl.pallas_call(
        functools.partial(_combine_kernel, tm=tm),
        grid=(nt,),
        in_specs=[pl.BlockSpec((1, 1, TOP_K * tm), lambda i: (i, 0, 0), memory_space=pltpu.SMEM),
                  pl.BlockSpec((1, 1, TOP_K * tm), lambda i: (jnp.minimum(i + 1, nt - 1), 0, 0),
                               memory_space=pltpu.SMEM),
                  pl.BlockSpec(memory_space=pl.ANY),
                  pl.BlockSpec((tm, D), lambda i: (i, 0)),
                  pl.BlockSpec((tm, LANES), lambda i: (i, 0)),
                  pl.BlockSpec((1, D), lambda i: (0, 0)),
                  pl.BlockSpec((1, D), lambda i: (0, 0))],
        out_specs=pl.BlockSpec((tm, D), lambda i: (i, 0)),
        out_shape=jax.ShapeDtypeStruct((T, D), F32),
        scratch_shapes=[pltpu.VMEM((2, TOP_K * tm, D), F32), pltpu.SemaphoreType.DMA((2,))],
        compiler_params=_cparams("arbitrary"),
        name="moe_combine",
    )(dest3, dest3, ys, x, gate, g, b)


def _dispatch(idx2, blk, n_blocks):
    T = idx2.shape[0]
    n_assign = T * TOP_K
    expert = idx2.reshape(-1)
    onehot = (expert[:, None] == jnp.arange(N_EXPERTS, dtype=jnp.int32)[None, :]).astype(jnp.int32)
    csum = jnp.cumsum(onehot, axis=0)
    rank = jnp.sum(onehot * (csum - 1), axis=1)
    counts = csum[-1]
    padded = (counts + blk - 1) // blk * blk
    padded_end = jnp.cumsum(padded)
    dest = (padded_end - padded)[expert] + rank
    token = jnp.arange(n_assign, dtype=jnp.int32) // TOP_K
    slot_token = jnp.zeros((n_blocks * blk,), jnp.int32).at[dest].set(token)
    block_expert = jnp.minimum(
        jnp.searchsorted(padded_end, jnp.arange(n_blocks, dtype=jnp.int32) * blk, side="right"),
        N_EXPERTS - 1).astype(jnp.int32)
    n_valid = (padded_end[-1:] // blk).astype(jnp.int32)
    block_ids = jnp.arange(n_blocks, dtype=jnp.int32)
    block_expert = jnp.where(block_ids < n_valid[0], block_expert,
                             block_expert[jnp.maximum(n_valid[0] - 1, 0)])
    return dest.reshape(T, TOP_K).astype(jnp.int32), slot_token, block_expert, n_valid


def _tile(n, pref):
    t = min(n, pref)
    while n % t:
        t //= 2
    return t


def kernel(x, w_in, w_out, hg_lb_logits, hg_norm_g, ml_conv_w, ml_conv_b, ml_b_i, ml_b_f, ml_norm_g,
           ln_mix_g, ln_mix_b, ln_ffn_g, ln_ffn_b, ffn_w_gate, ffn_w_up, ffn_w_down,
           moe_w_router, moe_w_gate, moe_w_up, moe_w_down):
    B, S, D = x.shape
    T = B * S
    depth = w_in.shape[0]
    hg_w = HG_HEADS * HG_D
    main_cols = 4 * hg_w + 2 * ML_HEADS * ML_DQK + 2 * ML_HEADS * ML_DV

    p = jax.nn.softmax(hg_lb_logits.astype(F32), axis=0)
    lower_bounds = jnp.cumsum(p, axis=0) - p[0]

    rows = _tile(S, 256)
    tm_proj = _tile(T, 1024)
    tm = _tile(T, 512)
    blk = _tile(T, 512)
    n_blocks = -(-(T * TOP_K + N_EXPERTS * (blk - 1)) // blk)
    wt = 1024

    f_dense = ffn_w_gate.shape[2]
    f_moe = moe_w_gate.shape[3]
    tf_dense = 512
    f_dense_p = -(-f_dense // tf_dense) * tf_dense

    xt = x.reshape(T, D).astype(F32)
    for layer in range(depth):
        w_main = _cast_bf16(w_in, layer, D, main_cols, wt, wt)
        w_gates = jnp.pad(w_in[layer, :, main_cols:], ((0, 0), (0, LANES - 2 * ML_HEADS))).astype(BF16)
        proj, gates = _proj(xt, w_main, w_gates, tm_proj, 1024)

        hg = _hgrn2(proj, lower_bounds[layer][None, :], hg_norm_g[layer][None, :].astype(F32), B, S, rows,
                    HG_HEADS // 2)
        gate_bias = jnp.pad(jnp.concatenate([ml_b_i[layer], ml_b_f[layer]]).astype(F32),
                            (0, LANES - 2 * ML_HEADS))[None, :]
        ml = _mlstm(proj, gates, ml_conv_w[layer].astype(F32), ml_conv_b[layer][None, :].astype(F32),
                    gate_bias, ml_norm_g[layer][None, :].astype(F32), B, S, rows, 4 * hg_w)

        moe = layer % 2 == 1
        j = layer // 2
        w_router = None
        if moe:
            wr = jnp.pad(moe_w_router[j].astype(F32), ((0, 0), (0, LANES - N_EXPERTS)))
            wr_hi = wr.astype(BF16)
            w_router = jnp.concatenate([wr_hi, (wr - wr_hi.astype(F32)).astype(BF16)], axis=1)
        res = _outproj(hg, ml, _cast_bf16(w_out, layer, D, D, wt, wt), xt,
                       ln_mix_g[layer][None, :], ln_mix_b[layer][None, :], w_router, tm)
        g2 = ln_ffn_g[layer][None, :]
        b2 = ln_ffn_b[layer][None, :]
        if not moe:
            (x1,) = res
            wg = _cast_bf16(ffn_w_gate, j, D, f_dense_p, wt, tf_dense)
            wu = _cast_bf16(ffn_w_up, j, D, f_dense_p, wt, tf_dense)
            wd = _cast_bf16(ffn_w_down, j, f_dense_p, D, tf_dense, wt)
            xt = _ffn(x1, wg, wu, wd, g2, b2, tm, tf_dense)
        else:
            x1, idx, gate = res
            dest, slot_token, block_expert, n_valid = _dispatch(idx[:, :TOP_K], blk, n_blocks)
            n_layers_moe = moe_w_gate.shape[0]
            wg = _cast_bf16(moe_w_gate.reshape(n_layers_moe, N_EXPERTS * D, f_moe), j,
                            N_EXPERTS * D, f_moe, wt, wt).reshape(N_EXPERTS, D, f_moe)
            wu = _cast_bf16(moe_w_up.reshape(n_layers_moe, N_EXPERTS * D, f_moe), j,
                            N_EXPERTS * D, f_moe, wt, wt).reshape(N_EXPERTS, D, f_moe)
            wd = _cast_bf16(moe_w_down.reshape(n_layers_moe, N_EXPERTS * f_moe, D), j,
                            N_EXPERTS * f_moe, D, wt, wt).reshape(N_EXPERTS, f_moe, D)
            ys = _moe_ffn(block_expert, n_valid, slot_token, x1, wg, wu, wd, blk, 1024)
            xt = _combine(dest, ys, x1, gate, g2, b2, _tile(T, 256))
    return xt.reshape(B, S, D).astype(x.dtype)
```

```python
import functools

import jax
import jax.numpy as jnp
import numpy as np
from jax import lax
from jax.experimental import pallas as pl
from jax.experimental.pallas import tpu as pltpu

F32 = jnp.float32
BF16 = jnp.bfloat16
HIGHEST = lax.Precision.HIGHEST

MODEL_DEPTH = 4
HG_HEADS = 8
HG_D = 128
ML_HEADS = 4
ML_DQK = 128
ML_DV = 256
CONV_W = 4
N_EXPERTS = 8
TOP_K = 2
ALPHA = (2 * MODEL_DEPTH) ** 0.25
LN_EPS = 1e-5
RMS_EPS = 1e-6
NEG_BIG = -1e30
F_MIN = 1e-30

LANES = 128
SUBLANES = 8
VMEM_LIMIT_BYTES = 56 * 1024 * 1024
SIDE_CAST_BLOCK_BYTES = 4 * 1024 * 1024

CHUNK = 64
CONV_PAD = SUBLANES


def _cparams(*sem):
    return pltpu.CompilerParams(dimension_semantics=sem, vmem_limit_bytes=VMEM_LIMIT_BYTES)


def _sigmoid(x):
    return 1.0 / (1.0 + jnp.exp(-x))


def _layer_norm(y, g, b):
    mu = jnp.mean(y, axis=-1, keepdims=True)
    yc = y - mu
    var = jnp.mean(yc * yc, axis=-1, keepdims=True)
    return yc * lax.rsqrt(var + LN_EPS) * g + b


def _cast_kernel(x_ref, o_ref, *, rows_in, cols_in, mask):
    x = x_ref[...]
    if mask:
        tr, tc = x.shape
        r = pl.program_id(0) * tr + lax.broadcasted_iota(jnp.int32, x.shape, 0)
        c = pl.program_id(1) * tc + lax.broadcasted_iota(jnp.int32, x.shape, 1)
        x = jnp.where((r < rows_in) & (c < cols_in), x, 0.0)
    o_ref[...] = x.astype(BF16)


def _cast_bf16(w, layer, rows_out, cols_out, tr, tc):
    _, R, C = w.shape
    mask = rows_out > R or cols_out > C
    return pl.pallas_call(
        functools.partial(_cast_kernel, rows_in=R, cols_in=C, mask=mask),
        grid=(rows_out // tr, cols_out // tc),
        in_specs=[pl.BlockSpec((None, tr, tc), lambda i, j: (layer, i, j))],
        out_specs=pl.BlockSpec((tr, tc), lambda i, j: (i, j)),
        out_shape=jax.ShapeDtypeStruct((rows_out, cols_out), BF16),
        compiler_params=_cparams("parallel", "parallel"),
        name="cast_bf16",
    )(w)


def _proj_kernel(x_ref, w_ref, wgate_ref, o_ref, ogate_ref):
    xb = x_ref[...].astype(BF16)
    o_ref[...] = jnp.dot(xb, w_ref[...], preferred_element_type=F32)

    @pl.when(pl.program_id(1) == 0)
    def _():
        ogate_ref[...] = jnp.dot(xb, wgate_ref[...], preferred_element_type=F32)


def _proj(x, w, w_gates, tm, tn):
    T, D = x.shape
    N = w.shape[1]
    return pl.pallas_call(
        _proj_kernel,
        grid=(T // tm, N // tn),
        in_specs=[pl.BlockSpec((tm, D), lambda i, j: (i, 0)),
                  pl.BlockSpec((D, tn), lambda i, j: (0, j)),
                  pl.BlockSpec((D, LANES), lambda i, j: (0, 0))],
        out_specs=[pl.BlockSpec((tm, tn), lambda i, j: (i, j)),
                   pl.BlockSpec((tm, LANES), lambda i, j: (i, 0))],
        out_shape=[jax.ShapeDtypeStruct((T, N), F32), jax.ShapeDtypeStruct((T, LANES), F32)],
        compiler_params=_cparams("parallel", "arbitrary"),
        name="in_proj",
    )(x, w, w_gates)


_HG_LEVELS = (32, 16, 8, 4, 2, 1)
_HG_DIAG = len(_HG_LEVELS)


def _hgrn2_tables():
    t = np.arange(CHUNK)[:, None]
    s = np.arange(CHUNK)[None, :]
    level = np.full((CHUNK, CHUNK), -1, np.int32)
    level[t == s] = _HG_DIAG
    tril = (s <= t).astype(np.float32)
    mats = [tril]
    for li, m in enumerate(_HG_LEVELS):
        mask = (t // (2 * m) == s // (2 * m)) & (t % (2 * m) >= m) & (s % (2 * m) < m)
        assert not (level[mask] >= 0).any()
        level[mask] = li
        if m < SUBLANES:
            mid = (t // (2 * m)) * (2 * m) + m
            mats.append(tril - (s <= mid - 1).astype(np.float32))
    assert (level[s < t] >= 0).all() and (level[s > t] == -1).all()
    return np.concatenate(mats, axis=0), level


def _hgrn2_kernel(q_ref, f_ref, i_ref, g_ref, lb_ref, ng_ref, w_ref, lv_ref, *rest, n_chunks, n_heads, side_cast):
    if side_cast:
        cast_in_ref, o_ref, cast_out_ref, st_ref = rest
        cast_out_ref[...] = cast_in_ref[...].astype(BF16)
    else:
        o_ref, st_ref = rest
    @pl.when(pl.program_id(2) == 0)
    def _():
        st_ref[...] = jnp.zeros_like(st_ref)

    w = w_ref[...]
    lv = lv_ref[...]
    ones_bf = jnp.ones((HG_D, CHUNK), BF16)
    nt = (((1,), (1,)), ((), ()))
    units = [(hh, c) for hh in range(n_heads) for c in range(n_chunks)]
    rows_of = {u: slice(u[1] * CHUNK, (u[1] + 1) * CHUNK) for u in units}
    lanes_of = {u: slice(u[0] * HG_D, (u[0] + 1) * HG_D) for u in units}

    q, k, v_bf, wb = {}, {}, {}, {}
    for u in units:
        lb = lb_ref[:, lanes_of[u]]
        one_m_lb = 1.0 - lb
        qp = q_ref[rows_of[u], lanes_of[u]]
        fp = f_ref[rows_of[u], lanes_of[u]]
        q[u] = qp * _sigmoid(qp)
        e_f = jnp.exp(-fp)
        sig_f = 1.0 / (1.0 + e_f)
        log_f = jnp.log2(jnp.maximum(lb + one_m_lb * sig_f, F_MIN))
        k[u] = one_m_lb * (e_f * sig_f)
        v_bf[u] = i_ref[rows_of[u], lanes_of[u]].astype(BF16)
        lf_hi = log_f.astype(BF16)
        lf_lo = (log_f - lf_hi.astype(F32)).astype(BF16)
        wb[u] = (jnp.dot(w, lf_hi, preferred_element_type=F32)
                 + jnp.dot(w, lf_lo, preferred_element_type=F32))

    p, qd, upd, dec = {}, {}, {}, {}
    for u in units:
        b = wb[u][0:CHUNK, :]
        b_last = b[CHUNK - 1:CHUNK, :]
        qd[u] = (q[u] * jnp.exp2(b)).astype(BF16)
        kd = k[u] * jnp.exp2(b_last - b)
        upd[u] = jnp.dot(i_ref[rows_of[u], lanes_of[u]].T.astype(BF16), kd.astype(BF16),
                         preferred_element_type=F32)
        dec[u] = jnp.exp2(b_last)
        pc = jnp.where(lv == _HG_DIAG,
                       jnp.dot((q[u] * k[u]).astype(BF16), ones_bf, preferred_element_type=F32), 0.0)
        small = 0
        for li, m in enumerate(_HG_LEVELS):
            if m >= SUBLANES:
                zero = jnp.zeros((m, HG_D), F32)
                qs, ks = [], []
                for lo in range(0, CHUNK, 2 * m):
                    mid = lo + m
                    ref = b[mid - 1:mid, :]
                    ks += [k[u][lo:mid, :] * jnp.exp2(ref - b[lo:mid, :]), zero]
                    qs += [zero, q[u][mid:mid + m, :] * jnp.exp2(b[mid:mid + m, :] - ref)]
                qt = jnp.concatenate(qs, axis=0)
                kt = jnp.concatenate(ks, axis=0)
            else:
                small += 1
                d = wb[u][small * CHUNK:(small + 1) * CHUNK, :]
                qt = q[u] * jnp.exp2(jnp.minimum(d, 0.0))
                kt = k[u] * jnp.exp2(jnp.minimum(-d, 0.0))
            s_l = lax.dot_general(qt.astype(BF16), kt.astype(BF16), nt, preferred_element_type=F32)
            pc = jnp.where(lv == li, s_l, pc)
        p[u] = pc.astype(BF16)

    o = {}
    for hh in range(n_heads):
        st = st_ref[hh]
        for c in range(n_chunks):
            u = (hh, c)
            o[u] = lax.dot_general(qd[u], st.astype(BF16), nt, preferred_element_type=F32)
            st = st * dec[u] + upd[u]
        st_ref[hh] = st

    for u in units:
        oc = o[u] + jnp.dot(p[u], v_bf[u], preferred_element_type=F32)
        gp = g_ref[rows_of[u], lanes_of[u]]
        on = oc * lax.rsqrt(jnp.mean(oc * oc, axis=-1, keepdims=True) + RMS_EPS) * ng_ref[:, lanes_of[u]]
        o_ref[rows_of[u], lanes_of[u]] = (on * (gp * _sigmoid(gp))).astype(o_ref.dtype)


def _side_cast_rows(cast_src, n_steps):
    if cast_src is None:
        return 0
    _, R, C = cast_src[0].shape
    rc = R // n_steps
    ok = R % n_steps == 0 and rc % 16 == 0 and C % LANES == 0 and rc * C * 4 <= SIDE_CAST_BLOCK_BYTES
    return rc if ok else 0


def _hgrn2(proj, lb, ng, B, S, rows, heads_per_step, cast_src=None):
    T = B * S
    ns = S // rows
    n_chunks = rows // CHUNK
    H = HG_HEADS
    hp = heads_per_step
    wd = hp * HG_D
    w_np, level_np = _hgrn2_tables()
    w = jnp.asarray(w_np, BF16)
    level = jnp.asarray(level_np, jnp.int32)
    rc = _side_cast_rows(cast_src, B * (H // hp) * ns)

    def col_spec(group):
        return pl.BlockSpec((rows, wd), lambda b, h, j, g=group: (b * ns + j, g * (H // hp) + h))

    vec_spec = pl.BlockSpec((1, wd), lambda b, h, j: (0, h))
    const = lambda b, h, j: (0, 0)
    in_specs = [col_spec(0), col_spec(1), col_spec(2), col_spec(3), vec_spec, vec_spec,
                pl.BlockSpec(w.shape, const), pl.BlockSpec(level.shape, const)]
    out_specs = [pl.BlockSpec((rows, wd), lambda b, h, j: (b * ns + j, h))]
    out_shape = [jax.ShapeDtypeStruct((T, H * HG_D), BF16)]
    args = [proj, proj, proj, proj, lb, ng, w, level]
    if rc:
        w3d, w_layer = cast_src
        step = lambda b, h, j: ((b * (H // hp) + h) * ns + j, 0)
        in_specs.append(pl.BlockSpec((None, rc, w3d.shape[2]), lambda b, h, j: (w_layer,) + step(b, h, j)))
        out_specs.append(pl.BlockSpec((rc, w3d.shape[2]), step))
        out_shape.append(jax.ShapeDtypeStruct(w3d.shape[1:], BF16))
        args.append(w3d)
    res = pl.pallas_call(
        functools.partial(_hgrn2_kernel, n_chunks=n_chunks, n_heads=hp, side_cast=bool(rc)),
        grid=(B, H // hp, ns),
        in_specs=in_specs, out_specs=out_specs, out_shape=out_shape,
        scratch_shapes=[pltpu.VMEM((hp, HG_D, HG_D), F32)],
        compiler_params=_cparams("parallel", "parallel", "arbitrary"),
        name="hgrn2",
    )(*args)
    return res[0], (res[1] if rc else None)


def _mlstm_kernel(q_ref, k_ref, v_ref, og_ref, gt_ref, cw_ref, cb_ref, gb_ref, ng_ref, *rest,
                  n_chunks, side_cast):
    if side_cast:
        cast_in_ref, o_ref, cast_out_ref, cq_ref, ck_ref, c_ref, n_ref, m_ref = rest
        cast_out_ref[...] = cast_in_ref[...].astype(BF16)
    else:
        o_ref, cq_ref, ck_ref, c_ref, n_ref, m_ref = rest
    rows = n_chunks * CHUNK
    W = ML_HEADS * ML_DQK

    @pl.when(pl.program_id(1) == 0)
    def _():
        cq_ref[0:CONV_PAD, :] = jnp.zeros((CONV_PAD, W), F32)
        ck_ref[0:CONV_PAD, :] = jnp.zeros((CONV_PAD, W), F32)
        c_ref[...] = jnp.zeros_like(c_ref)
        n_ref[...] = jnp.zeros_like(n_ref)
        m_ref[...] = jnp.zeros_like(m_ref)

    cq_ref[CONV_PAD:CONV_PAD + rows, :] = q_ref[...]
    ck_ref[CONV_PAD:CONV_PAD + rows, :] = k_ref[...]

    row = lax.broadcasted_iota(jnp.int32, (CHUNK, CHUNK), 0)
    col = lax.broadcasted_iota(jnp.int32, (CHUNK, CHUNK), 1)
    causal = col <= row
    tril = causal.astype(F32)
    lane = lax.broadcasted_iota(jnp.int32, (CHUNK, LANES), 1)
    k_scale = ML_DQK ** -0.5

    for c in range(n_chunks):
        r0 = c * CHUNK
        pre = gt_ref[r0:r0 + CHUNK, :] + gb_ref[...]
        log_f = jnp.minimum(pre, 0.0) - jnp.log1p(jnp.exp(-jnp.abs(pre)))
        bcum = jnp.dot(tril, log_f, precision=HIGHEST, preferred_element_type=F32)
        x = jnp.where(lane < ML_HEADS, pre, bcum)
        xt = x.T

        for h in range(ML_HEADS):
            cs = slice(h * ML_DQK, (h + 1) * ML_DQK)
            vs = slice(h * ML_DV, (h + 1) * ML_DV)

            def conv(src_ref, w_off):
                acc = cb_ref[0:1, w_off + h * ML_DQK:w_off + (h + 1) * ML_DQK]
                for j in range(CONV_W):
                    start = CONV_PAD + r0 - (CONV_W - 1) + j
                    acc = acc + (cw_ref[j:j + 1, w_off + h * ML_DQK:w_off + (h + 1) * ML_DQK]
                                 * src_ref[start:start + CHUNK, cs])
                return acc * _sigmoid(acc)

            q = conv(cq_ref, 0)
            k = conv(ck_ref, W) * k_scale
            v = v_ref[r0:r0 + CHUNK, vs]
            q_bf = q.astype(BF16)
            v_bf = v.astype(BF16)

            li_c = x[:, h:h + 1]
            b_c = x[:, ML_HEADS + h:ML_HEADS + h + 1]
            li_r = xt[h:h + 1, :]
            b_r = xt[ML_HEADS + h:ML_HEADS + h + 1, :]
            b_last = b_c[CHUNK - 1:CHUNK, :]
            m_prev = m_ref[h][0:1, 0:1]
            c_state = c_ref[h]
            n_state = n_ref[h][0:1, :]

            log_intra = jnp.where(causal, b_c - b_r + li_r, NEG_BIG)
            log_inter = b_c + m_prev
            m_t = jnp.maximum(log_inter, jnp.max(log_intra, axis=-1, keepdims=True))
            w_intra = jnp.exp(log_intra - m_t)
            w_inter = jnp.exp(log_inter - m_t)
            scores = lax.dot_general(q_bf, k.astype(BF16), (((1,), (1,)), ((), ())),
                                     preferred_element_type=F32) * w_intra
            num = (jnp.dot(scores.astype(BF16), v_bf, preferred_element_type=F32)
                   + w_inter * jnp.dot(q_bf, c_state.astype(BF16), preferred_element_type=F32))
            den = (jnp.sum(scores, axis=-1, keepdims=True)
                   + w_inter * jnp.sum(q * n_state, axis=-1, keepdims=True))
            hh = num / jnp.maximum(jnp.abs(den), jnp.exp(-m_t))

            m_new = m_t[CHUNK - 1:CHUNK, :]
            w_write = jnp.exp(b_last - b_c + li_c - m_new)
            w_carry = jnp.exp(b_last + m_prev - m_new)
            kw = k * w_write
            c_ref[h] = w_carry * c_state + jnp.dot(kw.T.astype(BF16), v_bf,
                                                   preferred_element_type=F32)
            n_new = w_carry * n_state + jnp.sum(kw, axis=0, keepdims=True)
            n_ref[h] = jnp.broadcast_to(n_new, (SUBLANES, ML_DQK))
            m_ref[h] = jnp.broadcast_to(m_new, (SUBLANES, LANES))

            hn = hh * lax.rsqrt(jnp.mean(hh * hh, axis=-1, keepdims=True) + RMS_EPS) * ng_ref[0:1, vs]
            og = og_ref[r0:r0 + CHUNK, vs]
            o_ref[r0:r0 + CHUNK, vs] = (_sigmoid(og) * hn).astype(o_ref.dtype)

    cq_ref[0:CONV_PAD, :] = cq_ref[rows:rows + CONV_PAD, :]
    ck_ref[0:CONV_PAD, :] = ck_ref[rows:rows + CONV_PAD, :]


def _mlstm(proj, gates, conv_w, conv_b, gate_bias, ng, B, S, rows, col0, cast_src=None):
    T = B * S
    ns = S // rows
    n_chunks = rows // CHUNK
    W = ML_HEADS * ML_DQK
    V = ML_HEADS * ML_DV
    qb = col0 // W
    vb = (col0 + 2 * W) // V
    rc = _side_cast_rows(cast_src, B * ns)

    def rowmap(cb):
        return lambda b, j, cb=cb: (b * ns + j, cb)

    const = lambda b, j: (0, 0)
    in_specs = [pl.BlockSpec((rows, W), rowmap(qb)),
                pl.BlockSpec((rows, W), rowmap(qb + 1)),
                pl.BlockSpec((rows, V), rowmap(vb)),
                pl.BlockSpec((rows, V), rowmap(vb + 1)),
                pl.BlockSpec((rows, LANES), rowmap(0)),
                pl.BlockSpec((CONV_W, 2 * W), const),
                pl.BlockSpec((1, 2 * W), const),
                pl.BlockSpec((1, LANES), const),
                pl.BlockSpec((1, V), const)]
    out_specs = [pl.BlockSpec((rows, V), rowmap(0))]
    out_shape = [jax.ShapeDtypeStruct((T, V), BF16)]
    args = [proj, proj, proj, proj, gates, conv_w, conv_b, gate_bias, ng]
    if rc:
        w3d, w_layer = cast_src
        in_specs.append(pl.BlockSpec((None, rc, w3d.shape[2]), lambda b, j: (w_layer, b * ns + j, 0)))
        out_specs.append(pl.BlockSpec((rc, w3d.shape[2]), rowmap(0)))
        out_shape.append(jax.ShapeDtypeStruct(w3d.shape[1:], BF16))
        args.append(w3d)
    res = pl.pallas_call(
        functools.partial(_mlstm_kernel, n_chunks=n_chunks, side_cast=bool(rc)),
        grid=(B, ns),
        in_specs=in_specs, out_specs=out_specs, out_shape=out_shape,
        scratch_shapes=[pltpu.VMEM((rows + 2 * CONV_PAD, W), F32),
                        pltpu.VMEM((rows + 2 * CONV_PAD, W), F32),
                        pltpu.VMEM((ML_HEADS, ML_DQK, ML_DV), F32),
                        pltpu.VMEM((ML_HEADS, SUBLANES, ML_DQK), F32),
                        pltpu.VMEM((ML_HEADS, SUBLANES, LANES), F32)],
        compiler_params=_cparams("parallel", "arbitrary"),
        name="mlstm",
    )(*args)
    return res[0], (res[1] if rc else None)


def _top2(logits):
    lane = lax.broadcasted_iota(jnp.int32, logits.shape, 1)
    lg = jnp.where(lane < N_EXPERTS, logits, -jnp.inf)
    m0 = jnp.max(lg, axis=-1, keepdims=True)
    i0 = jnp.min(jnp.where(lg == m0, lane, LANES), axis=-1, keepdims=True)
    lg1 = jnp.where(lane == i0, -jnp.inf, lg)
    m1 = jnp.max(lg1, axis=-1, keepdims=True)
    i1 = jnp.min(jnp.where(lg1 == m1, lane, LANES), axis=-1, keepdims=True)
    e1 = jnp.exp(m1 - m0)
    g0 = 1.0 / (1.0 + e1)
    g1 = e1 / (1.0 + e1)
    idx = jnp.where(lane == 0, i0, jnp.where(lane == 1, i1, 0))
    gate = jnp.where(lane == 0, g0, jnp.where(lane == 1, g1, 0.0))
    return idx, gate


def _outproj_kernel(hg_ref, ml_ref, w_ref, x_ref, g_ref, b_ref, *rest, route):
    if route:
        wr_ref, o_ref, idx_ref, gate_ref = rest
    else:
        (o_ref,) = rest
    half = hg_ref.shape[1]
    tm = x_ref.shape[0]
    parts = [slice(0, tm // 2), slice(tm // 2, tm)]
    mixed = [jnp.dot(hg_ref[r, :], w_ref[0:half, :], preferred_element_type=F32)
             + jnp.dot(ml_ref[r, :], w_ref[half:, :], preferred_element_type=F32) for r in parts]
    y = jnp.concatenate([_layer_norm(ALPHA * x_ref[r, :] + m, g_ref[...], b_ref[...])
                         for r, m in zip(parts, mixed)], axis=0)
    o_ref[...] = y
    if route:
        y_hi = y.astype(BF16)
        y_lo = (y - y_hi.astype(F32)).astype(BF16)
        r_hi = jnp.dot(y_hi, wr_ref[...], preferred_element_type=F32)
        r_lo = jnp.dot(y_lo, wr_ref[:, 0:LANES], preferred_element_type=F32)
        logits = r_hi[:, 0:LANES] + r_hi[:, LANES:] + r_lo
        idx, gate = _top2(logits)
        idx_ref[...] = idx
        gate_ref[...] = gate


def _outproj(hg, ml, w, x, g, b, w_router, tm):
    T, D = x.shape
    half = hg.shape[1]
    route = w_router is not None
    row = lambda i: (i, 0)
    const = lambda i: (0, 0)
    in_specs = [pl.BlockSpec((tm, half), row), pl.BlockSpec((tm, half), row),
                pl.BlockSpec((D, D), const), pl.BlockSpec((tm, D), row),
                pl.BlockSpec((1, D), const), pl.BlockSpec((1, D), const)]
    out_specs = [pl.BlockSpec((tm, D), row)]
    out_shape = [jax.ShapeDtypeStruct((T, D), F32)]
    args = [hg, ml, w, x, g, b]
    if route:
        in_specs.append(pl.BlockSpec((D, 2 * LANES), const))
        out_specs += [pl.BlockSpec((tm, LANES), row), pl.BlockSpec((tm, LANES), row)]
        out_shape += [jax.ShapeDtypeStruct((T, LANES), jnp.int32),
                      jax.ShapeDtypeStruct((T, LANES), F32)]
        args.append(w_router)
    return pl.pallas_call(
        functools.partial(_outproj_kernel, route=route),
        grid=(T // tm,),
        in_specs=in_specs, out_specs=out_specs, out_shape=out_shape,
        compiler_params=_cparams("parallel"),
        name="out_proj_route" if route else "out_proj",
    )(*args)


def _ffn_kernel(x_ref, wg_ref, wu_ref, wd_ref, g_ref, b_ref, o_ref, xb_ref):
    f = pl.program_id(1)

    @pl.when(f == 0)
    def _():
        xb_ref[...] = x_ref[...].astype(BF16)
        o_ref[...] = jnp.zeros_like(o_ref)

    xb = xb_ref[...]
    gt = jnp.dot(xb, wg_ref[...], preferred_element_type=F32)
    up = jnp.dot(xb, wu_ref[...], preferred_element_type=F32)
    h = (gt * _sigmoid(gt) * up).astype(BF16)
    o_ref[...] += jnp.dot(h, wd_ref[...], preferred_element_type=F32)

    @pl.when(f == pl.num_programs(1) - 1)
    def _():
        o_ref[...] = _layer_norm(ALPHA * x_ref[...] + o_ref[...], g_ref[...], b_ref[...])


def _ffn(x, wg, wu, wd, g, b, tm, tf):
    T, D = x.shape
    Fp = wg.shape[1]
    return pl.pallas_call(
        _ffn_kernel,
        grid=(T // tm, Fp // tf),
        in_specs=[pl.BlockSpec((tm, D), lambda i, f: (i, 0), pipeline_mode=pl.Buffered(1)),
                  pl.BlockSpec((D, tf), lambda i, f: (0, f)),
                  pl.BlockSpec((D, tf), lambda i, f: (0, f)),
                  pl.BlockSpec((tf, D), lambda i, f: (f, 0)),
                  pl.BlockSpec((1, D), lambda i, f: (0, 0)),
                  pl.BlockSpec((1, D), lambda i, f: (0, 0))],
        out_specs=pl.BlockSpec((tm, D), lambda i, f: (i, 0)),
        out_shape=jax.ShapeDtypeStruct((T, D), F32),
        scratch_shapes=[pltpu.VMEM((tm, D), BF16)],
        compiler_params=_cparams("parallel", "arbitrary"),
        name="dense_ffn",
    )(x, wg, wu, wd, g, b)


def _gather_rows(src_hbm, idx_ref, dst_ref, sem, n, *, wait):
    def body(r, c):
        cp = pltpu.make_async_copy(src_hbm.at[pl.ds(idx_ref[0, 0, r], 1), :],
                                   dst_ref.at[pl.ds(r, 1), :], sem)
        if wait:
            cp.wait()
        else:
            cp.start()
        return c
    lax.fori_loop(0, n, body, 0, unroll=8)


def _moe_kernel(be_ref, nv_ref, tok_ref, tok_next_ref, x_hbm, wg_ref, wu_ref, wd_ref, o_ref,
                xg_ref, xb_ref, sem, *, blk):
    del be_ref
    i = pl.program_id(0)
    f = pl.program_id(1)
    n_valid = nv_ref[0]
    live = i < n_valid
    slot = i % 2

    @pl.when(f == 0)
    def _():
        o_ref[...] = jnp.zeros_like(o_ref)

    @pl.when((i == 0) & (f == 0))
    def _():
        _gather_rows(x_hbm, tok_ref, xg_ref.at[0], sem.at[0], blk, wait=False)

    @pl.when(live & (f == 0))
    def _():
        _gather_rows(x_hbm, tok_ref, xg_ref.at[slot], sem.at[slot], blk, wait=True)
        xb_ref[...] = xg_ref[slot].astype(BF16)

    @pl.when((i + 1 < n_valid) & (f == 0))
    def _():
        _gather_rows(x_hbm, tok_next_ref, xg_ref.at[1 - slot], sem.at[1 - slot], blk, wait=False)

    @pl.when(live)
    def _():
        xb = xb_ref[...]
        gt = jnp.dot(xb, wg_ref[...], preferred_element_type=F32)
        up = jnp.dot(xb, wu_ref[...], preferred_element_type=F32)
        h = (gt * _sigmoid(gt) * up).astype(BF16)
        o_ref[...] += jnp.dot(h, wd_ref[...], preferred_element_type=F32)


def _moe_ffn(block_expert, n_valid, slot_token, x, wg, wu, wd, blk, tf):
    T, D = x.shape
    F = wg.shape[2]
    n_blocks = block_expert.shape[0]
    nf = F // tf
    tok3 = slot_token.reshape(n_blocks, 1, blk)

    def f_eff(i, f, nv):
        return jnp.where(i < nv[0], f, nf - 1)

    grid_spec = pltpu.PrefetchScalarGridSpec(
        num_scalar_prefetch=2,
        grid=(n_blocks, nf),
        in_specs=[pl.BlockSpec((1, 1, blk), lambda i, f, be, nv: (i, 0, 0), memory_space=pltpu.SMEM),
                  pl.BlockSpec((1, 1, blk), lambda i, f, be, nv: (jnp.minimum(i + 1, n_blocks - 1), 0, 0),
                               memory_space=pltpu.SMEM),
                  pl.BlockSpec(memory_space=pl.ANY),
                  pl.BlockSpec((None, D, tf), lambda i, f, be, nv: (be[i], 0, f_eff(i, f, nv))),
                  pl.BlockSpec((None, D, tf), lambda i, f, be, nv: (be[i], 0, f_eff(i, f, nv))),
                  pl.BlockSpec((None, tf, D), lambda i, f, be, nv: (be[i], f_eff(i, f, nv), 0))],
        out_specs=pl.BlockSpec((blk, D), lambda i, f, be, nv: (i, 0)),
        scratch_shapes=[pltpu.VMEM((2, blk, D), F32), pltpu.VMEM((blk, D), BF16),
                        pltpu.SemaphoreType.DMA((2,))],
    )
    return pl.pallas_call(
        functools.partial(_moe_kernel, blk=blk),
        grid_spec=grid_spec,
        out_shape=jax.ShapeDtypeStruct((n_blocks * blk, D), F32),
        compiler_params=_cparams("arbitrary", "arbitrary"),
        name="moe_ffn",
    )(block_expert, n_valid, tok3, tok3, x, wg, wu, wd)


def _combine_kernel(dest_ref, dest_next_ref, ys_hbm, x_ref, gate_ref, g_ref, b_ref, o_ref,
                    buf_ref, sem, *, tm):
    i = pl.program_id(0)
    slot = i % 2
    n = TOP_K * tm

    @pl.when(i == 0)
    def _():
        _gather_rows(ys_hbm, dest_ref, buf_ref.at[0], sem.at[0], n, wait=False)

    @pl.when(i + 1 < pl.num_programs(0))
    def _():
        _gather_rows(ys_hbm, dest_next_ref, buf_ref.at[1 - slot], sem.at[1 - slot], n, wait=False)

    _gather_rows(ys_hbm, dest_ref, buf_ref.at[slot], sem.at[slot], n, wait=True)
    gate = gate_ref[...]
    ff = buf_ref[slot, 0:tm, :] * gate[:, 0:1] + buf_ref[slot, tm:2 * tm, :] * gate[:, 1:2]
    o_ref[...] = _layer_norm(ALPHA * x_ref[...] + ff, g_ref[...], b_ref[...])


def _combine(dest, ys, x, gate, g, b, tm):
    T, D = x.shape
    nt = T // tm
    dest3 = dest.reshape(nt, tm, TOP_K).transpose(0, 2, 1).reshape(nt, 1, TOP_K * tm)
    return pl.pallas_call(
        functools.partial(_combine_kernel, tm=tm),
        grid=(nt,),
        in_specs=[pl.BlockSpec((1, 1, TOP_K * tm), lambda i: (i, 0, 0), memory_space=pltpu.SMEM),
                  pl.BlockSpec((1, 1, TOP_K * tm), lambda i: (jnp.minimum(i + 1, nt - 1), 0, 0),
                               memory_space=pltpu.SMEM),
                  pl.BlockSpec(memory_space=pl.ANY),
                  pl.BlockSpec((tm, D), lambda i: (i, 0)),
                  pl.BlockSpec((tm, LANES), lambda i: (i, 0)),
                  pl.BlockSpec((1, D), lambda i: (0, 0)),
                  pl.BlockSpec((1, D), lambda i: (0, 0))],
        out_specs=pl.BlockSpec((tm, D), lambda i: (i, 0)),
        out_shape=jax.ShapeDtypeStruct((T, D), F32),
        scratch_shapes=[pltpu.VMEM((2, TOP_K * tm, D), F32), pltpu.SemaphoreType.DMA((2,))],
        compiler_params=_cparams("arbitrary"),
        name="moe_combine",
    )(dest3, dest3, ys, x, gate, g, b)


def _dispatch(idx2, blk, n_blocks):
    T = idx2.shape[0]
    n_assign = T * TOP_K
    expert = idx2.reshape(-1)
    onehot = (expert[:, None] == jnp.arange(N_EXPERTS, dtype=jnp.int32)[None, :]).astype(jnp.int32)
    csum = jnp.cumsum(onehot, axis=0)
    rank = jnp.sum(onehot * (csum - 1), axis=1)
    counts = csum[-1]
    padded = (counts + blk - 1) // blk * blk
    padded_end = jnp.cumsum(padded)
    dest = (padded_end - padded)[expert] + rank
    token = jnp.arange(n_assign, dtype=jnp.int32) // TOP_K
    slot_token = jnp.zeros((n_blocks * blk,), jnp.int32).at[dest].set(token)
    block_expert = jnp.minimum(
        jnp.searchsorted(padded_end, jnp.arange(n_blocks, dtype=jnp.int32) * blk, side="right"),
        N_EXPERTS - 1).astype(jnp.int32)
    n_valid = (padded_end[-1:] // blk).astype(jnp.int32)
    block_ids = jnp.arange(n_blocks, dtype=jnp.int32)
    block_expert = jnp.where(block_ids < n_valid[0], block_expert,
                             block_expert[jnp.maximum(n_valid[0] - 1, 0)])
    return dest.reshape(T, TOP_K).astype(jnp.int32), slot_token, block_expert, n_valid


def _tile(n, pref):
    t = min(n, pref)
    while n % t:
        t //= 2
    return t


def kernel(x, w_in, w_out, hg_lb_logits, hg_norm_g, ml_conv_w, ml_conv_b, ml_b_i, ml_b_f, ml_norm_g,
           ln_mix_g, ln_mix_b, ln_ffn_g, ln_ffn_b, ffn_w_gate, ffn_w_up, ffn_w_down,
           moe_w_router, moe_w_gate, moe_w_up, moe_w_down):
    B, S, D = x.shape
    T = B * S
    depth = w_in.shape[0]
    hg_w = HG_HEADS * HG_D
    main_cols = 4 * hg_w + 2 * ML_HEADS * ML_DQK + 2 * ML_HEADS * ML_DV

    p = jax.nn.softmax(hg_lb_logits.astype(F32), axis=0)
    lower_bounds = jnp.cumsum(p, axis=0) - p[0]

    rows = _tile(S, 256)
    tm_proj = _tile(T, 1024)
    tm = _tile(T, 512)
    blk = _tile(T, 512)
    n_blocks = -(-(T * TOP_K + N_EXPERTS * (blk - 1)) // blk)
    wt = 1024

    f_dense = ffn_w_gate.shape[2]
    f_moe = moe_w_gate.shape[3]
    tf_dense = 512
    f_dense_p = -(-f_dense // tf_dense) * tf_dense

    w_gates_all = jnp.pad(lax.slice_in_dim(w_in, main_cols, w_in.shape[2], axis=2),
                          ((0, 0), (0, 0), (0, LANES - 2 * ML_HEADS))).astype(BF16)

    n_layers_moe = moe_w_gate.shape[0]
    moe_f32 = {"gate": moe_w_gate.reshape(n_layers_moe, N_EXPERTS * D, f_moe),
               "up": moe_w_up.reshape(n_layers_moe, N_EXPERTS * D, f_moe),
               "down": moe_w_down.reshape(n_layers_moe, N_EXPERTS * f_moe, D)}
    moe_bf16 = {}

    def moe_weight(name, j):
        w = moe_bf16.get((name, j))
        if w is None:
            _, R, C = moe_f32[name].shape
            w = _cast_bf16(moe_f32[name], j, R, C, wt, wt)
        return w.reshape((N_EXPERTS, D, f_moe) if name != "down" else (N_EXPERTS, f_moe, D))

    xt = x.reshape(T, D).astype(F32)
    for layer in range(depth):
        moe = layer % 2 == 1
        j = layer // 2
        if moe:
            hg_cast, ml_cast = ("down", j), None
        elif layer + 1 < depth:
            hg_cast, ml_cast = ("gate", j), ("up", j)
        else:
            hg_cast, ml_cast = None, None

        w_main = w_in[layer, :, :main_cols].astype(BF16)
        proj, gates = _proj(xt, w_main, w_gates_all[layer], tm_proj, main_cols // 4)

        hg, side = _hgrn2(proj, lower_bounds[layer][None, :], hg_norm_g[layer][None, :].astype(F32), B, S, rows,
                          HG_HEADS // 2, cast_src=(moe_f32[hg_cast[0]], hg_cast[1]) if hg_cast else None)
        if side is not None:
            moe_bf16[hg_cast] = side
        gate_bias = jnp.pad(jnp.concatenate([ml_b_i[layer], ml_b_f[layer]]).astype(F32),
                            (0, LANES - 2 * ML_HEADS))[None, :]
        ml, side = _mlstm(proj, gates, ml_conv_w[layer].astype(F32), ml_conv_b[layer][None, :].astype(F32),
                          gate_bias, ml_norm_g[layer][None, :].astype(F32), B, S, rows, 4 * hg_w,
                          cast_src=(moe_f32[ml_cast[0]], ml_cast[1]) if ml_cast else None)
        if side is not None:
            moe_bf16[ml_cast] = side

        w_router = None
        if moe:
            wr = jnp.pad(moe_w_router[j].astype(F32), ((0, 0), (0, LANES - N_EXPERTS)))
            wr_hi = wr.astype(BF16)
            w_router = jnp.concatenate([wr_hi, (wr - wr_hi.astype(F32)).astype(BF16)], axis=1)
        res = _outproj(hg, ml, _cast_bf16(w_out, layer, D, D, wt, wt), xt,
                       ln_mix_g[layer][None, :], ln_mix_b[layer][None, :], w_router, tm)
        g2 = ln_ffn_g[layer][None, :]
        b2 = ln_ffn_b[layer][None, :]
        if not moe:
            (x1,) = res
            wg = _cast_bf16(ffn_w_gate, j, D, f_dense_p, wt, tf_dense)
            wu = _cast_bf16(ffn_w_up, j, D, f_dense_p, wt, tf_dense)
            wd = _cast_bf16(ffn_w_down, j, f_dense_p, D, tf_dense, wt)
            xt = _ffn(x1, wg, wu, wd, g2, b2, _tile(T, 1024), tf_dense)
        else:
            x1, idx, gate = res
            dest, slot_token, block_expert, n_valid = _dispatch(idx[:, :TOP_K], blk, n_blocks)
            ys = _moe_ffn(block_expert, n_valid, slot_token, x1, moe_weight("gate", j), moe_weight("up", j),
                          moe_weight("down", j), blk, 1024)
            xt = _combine(dest, ys, x1, gate, g2, b2, _tile(T, 256))
    return xt.reshape(B, S, D).astype(x.dtype)
```

```python
import functools

import jax
import jax.numpy as jnp
import numpy as np
from jax import lax
from jax.experimental import pallas as pl
from jax.experimental.pallas import tpu as pltpu

F32 = jnp.float32
BF16 = jnp.bfloat16
HIGHEST = lax.Precision.HIGHEST

MODEL_DEPTH = 4
HG_HEADS = 8
HG_D = 128
ML_HEADS = 4
ML_DQK = 128
ML_DV = 256
CONV_W = 4
N_EXPERTS = 8
TOP_K = 2
ALPHA = (2 * MODEL_DEPTH) ** 0.25
LN_EPS = 1e-5
RMS_EPS = 1e-6
NEG_BIG = -1e30
F_MIN = 1e-30

LANES = 128
SUBLANES = 8
VMEM_LIMIT_BYTES = 56 * 1024 * 1024
SIDE_CAST_BLOCK_BYTES = 4 * 1024 * 1024

CHUNK = 64
CONV_PAD = SUBLANES


def _cparams(*sem):
    return pltpu.CompilerParams(dimension_semantics=sem, vmem_limit_bytes=VMEM_LIMIT_BYTES)


def _sigmoid(x):
    return 1.0 / (1.0 + jnp.exp(-x))


def _layer_norm(y, g, b):
    mu = jnp.mean(y, axis=-1, keepdims=True)
    yc = y - mu
    var = jnp.mean(yc * yc, axis=-1, keepdims=True)
    return yc * lax.rsqrt(var + LN_EPS) * g + b


def _cast_kernel(x_ref, o_ref, *, rows_in, cols_in, mask):
    x = x_ref[...]
    if mask:
        tr, tc = x.shape
        r = pl.program_id(0) * tr + lax.broadcasted_iota(jnp.int32, x.shape, 0)
        c = pl.program_id(1) * tc + lax.broadcasted_iota(jnp.int32, x.shape, 1)
        x = jnp.where((r < rows_in) & (c < cols_in), x, 0.0)
    o_ref[...] = x.astype(BF16)


def _cast_bf16(w, layer, rows_out, cols_out, tr, tc):
    _, R, C = w.shape
    mask = rows_out > R or cols_out > C
    return pl.pallas_call(
        functools.partial(_cast_kernel, rows_in=R, cols_in=C, mask=mask),
        grid=(rows_out // tr, cols_out // tc),
        in_specs=[pl.BlockSpec((None, tr, tc), lambda i, j: (layer, i, j))],
        out_specs=pl.BlockSpec((tr, tc), lambda i, j: (i, j)),
        out_shape=jax.ShapeDtypeStruct((rows_out, cols_out), BF16),
        compiler_params=_cparams("parallel", "parallel"),
        name="cast_bf16",
    )(w)


def _proj_kernel(x_ref, w_ref, wgate_ref, o_ref, ogate_ref):
    xb = x_ref[...].astype(BF16)
    o_ref[...] = jnp.dot(xb, w_ref[...], preferred_element_type=F32)

    @pl.when(pl.program_id(1) == 0)
    def _():
        ogate_ref[...] = jnp.dot(xb, wgate_ref[...], preferred_element_type=F32)


def _proj(x, w, w_gates, tm, tn):
    T, D = x.shape
    N = w.shape[1]
    return pl.pallas_call(
        _proj_kernel,
        grid=(T // tm, N // tn),
        in_specs=[pl.BlockSpec((tm, D), lambda i, j: (i, 0)),
                  pl.BlockSpec((D, tn), lambda i, j: (0, j)),
                  pl.BlockSpec((D, LANES), lambda i, j: (0, 0))],
        out_specs=[pl.BlockSpec((tm, tn), lambda i, j: (i, j)),
                   pl.BlockSpec((tm, LANES), lambda i, j: (i, 0))],
        out_shape=[jax.ShapeDtypeStruct((T, N), F32), jax.ShapeDtypeStruct((T, LANES), F32)],
        compiler_params=_cparams("parallel", "arbitrary"),
        name="in_proj",
    )(x, w, w_gates)


_HG_LEVELS = (32, 16, 8, 4, 2, 1)
_HG_DIAG = len(_HG_LEVELS)


def _hgrn2_tables():
    t = np.arange(CHUNK)[:, None]
    s = np.arange(CHUNK)[None, :]
    level = np.full((CHUNK, CHUNK), -1, np.int32)
    level[t == s] = _HG_DIAG
    tril = (s <= t).astype(np.float32)
    mats = [tril]
    for li, m in enumerate(_HG_LEVELS):
        mask = (t // (2 * m) == s // (2 * m)) & (t % (2 * m) >= m) & (s % (2 * m) < m)
        assert not (level[mask] >= 0).any()
        level[mask] = li
        if m < SUBLANES:
            mid = (t // (2 * m)) * (2 * m) + m
            mats.append(tril - (s <= mid - 1).astype(np.float32))
    assert (level[s < t] >= 0).all() and (level[s > t] == -1).all()
    return np.concatenate(mats, axis=0), level


def _hgrn2_kernel(q_ref, f_ref, i_ref, g_ref, lb_ref, ng_ref, w_ref, lv_ref, *rest, n_chunks, n_heads, side_cast):
    if side_cast:
        cast_in_ref, o_ref, cast_out_ref, st_ref = rest
        cast_out_ref[...] = cast_in_ref[...].astype(BF16)
    else:
        o_ref, st_ref = rest
    @pl.when(pl.program_id(2) == 0)
    def _():
        st_ref[...] = jnp.zeros_like(st_ref)

    w = w_ref[...]
    lv = lv_ref[...]
    ones_bf = jnp.ones((HG_D, CHUNK), BF16)
    nt = (((1,), (1,)), ((), ()))
    units = [(hh, c) for hh in range(n_heads) for c in range(n_chunks)]
    rows_of = {u: slice(u[1] * CHUNK, (u[1] + 1) * CHUNK) for u in units}
    lanes_of = {u: slice(u[0] * HG_D, (u[0] + 1) * HG_D) for u in units}

    q, k, v_bf, wb = {}, {}, {}, {}
    for u in units:
        lb = lb_ref[:, lanes_of[u]]
        one_m_lb = 1.0 - lb
        qp = q_ref[rows_of[u], lanes_of[u]]
        fp = f_ref[rows_of[u], lanes_of[u]]
        q[u] = qp * _sigmoid(qp)
        e_f = jnp.exp(-fp)
        sig_f = 1.0 / (1.0 + e_f)
        log_f = jnp.log2(jnp.maximum(lb + one_m_lb * sig_f, F_MIN))
        k[u] = one_m_lb * (e_f * sig_f)
        v_bf[u] = i_ref[rows_of[u], lanes_of[u]].astype(BF16)
        lf_hi = log_f.astype(BF16)
        lf_lo = (log_f - lf_hi.astype(F32)).astype(BF16)
        wb[u] = (jnp.dot(w, lf_hi, preferred_element_type=F32)
                 + jnp.dot(w, lf_lo, preferred_element_type=F32))

    p, qd, upd, dec = {}, {}, {}, {}
    for u in units:
        b = wb[u][0:CHUNK, :]
        b_last = b[CHUNK - 1:CHUNK, :]
        qd[u] = (q[u] * jnp.exp2(b)).astype(BF16)
        kd = k[u] * jnp.exp2(b_last - b)
        upd[u] = jnp.dot(i_ref[rows_of[u], lanes_of[u]].T.astype(BF16), kd.astype(BF16),
                         preferred_element_type=F32)
        dec[u] = jnp.exp2(b_last)
        pc = jnp.where(lv == _HG_DIAG,
                       jnp.dot((q[u] * k[u]).astype(BF16), ones_bf, preferred_element_type=F32), 0.0)
        small = 0
        for li, m in enumerate(_HG_LEVELS):
            if m >= SUBLANES:
                zero = jnp.zeros((m, HG_D), F32)
                qs, ks = [], []
                for lo in range(0, CHUNK, 2 * m):
                    mid = lo + m
                    ref = b[mid - 1:mid, :]
                    ks += [k[u][lo:mid, :] * jnp.exp2(ref - b[lo:mid, :]), zero]
                    qs += [zero, q[u][mid:mid + m, :] * jnp.exp2(b[mid:mid + m, :] - ref)]
                qt = jnp.concatenate(qs, axis=0)
                kt = jnp.concatenate(ks, axis=0)
            else:
                small += 1
                d = wb[u][small * CHUNK:(small + 1) * CHUNK, :]
                qt = q[u] * jnp.exp2(jnp.minimum(d, 0.0))
                kt = k[u] * jnp.exp2(jnp.minimum(-d, 0.0))
            s_l = lax.dot_general(qt.astype(BF16), kt.astype(BF16), nt, preferred_element_type=F32)
            pc = jnp.where(lv == li, s_l, pc)
        p[u] = pc.astype(BF16)

    o = {}
    for hh in range(n_heads):
        st = st_ref[hh]
        for c in range(n_chunks):
            u = (hh, c)
            o[u] = lax.dot_general(qd[u], st.astype(BF16), nt, preferred_element_type=F32)
            st = st * dec[u] + upd[u]
        st_ref[hh] = st

    for u in units:
        oc = o[u] + jnp.dot(p[u], v_bf[u], preferred_element_type=F32)
        gp = g_ref[rows_of[u], lanes_of[u]]
        on = oc * lax.rsqrt(jnp.mean(oc * oc, axis=-1, keepdims=True) + RMS_EPS) * ng_ref[:, lanes_of[u]]
        o_ref[rows_of[u], lanes_of[u]] = (on * (gp * _sigmoid(gp))).astype(o_ref.dtype)


def _side_cast_rows(cast_src, n_steps):
    if cast_src is None:
        return 0
    _, R, C = cast_src[0].shape
    rc = R // n_steps
    ok = R % n_steps == 0 and rc % 16 == 0 and C % LANES == 0 and rc * C * 4 <= SIDE_CAST_BLOCK_BYTES
    return rc if ok else 0


def _hgrn2(proj, lb, ng, B, S, rows, heads_per_step, cast_src=None):
    T = B * S
    ns = S // rows
    n_chunks = rows // CHUNK
    H = HG_HEADS
    hp = heads_per_step
    wd = hp * HG_D
    w_np, level_np = _hgrn2_tables()
    w = jnp.asarray(w_np, BF16)
    level = jnp.asarray(level_np, jnp.int32)
    rc = _side_cast_rows(cast_src, B * (H // hp) * ns)

    def col_spec(group):
        return pl.BlockSpec((rows, wd), lambda b, h, j, g=group: (b * ns + j, g * (H // hp) + h))

    vec_spec = pl.BlockSpec((1, wd), lambda b, h, j: (0, h))
    const = lambda b, h, j: (0, 0)
    in_specs = [col_spec(0), col_spec(1), col_spec(2), col_spec(3), vec_spec, vec_spec,
                pl.BlockSpec(w.shape, const), pl.BlockSpec(level.shape, const)]
    out_specs = [pl.BlockSpec((rows, wd), lambda b, h, j: (b * ns + j, h))]
    out_shape = [jax.ShapeDtypeStruct((T, H * HG_D), BF16)]
    args = [proj, proj, proj, proj, lb, ng, w, level]
    if rc:
        w3d, w_layer = cast_src
        step = lambda b, h, j: ((b * (H // hp) + h) * ns + j, 0)
        in_specs.append(pl.BlockSpec((None, rc, w3d.shape[2]), lambda b, h, j: (w_layer,) + step(b, h, j)))
        out_specs.append(pl.BlockSpec((rc, w3d.shape[2]), step))
        out_shape.append(jax.ShapeDtypeStruct(w3d.shape[1:], BF16))
        args.append(w3d)
    res = pl.pallas_call(
        functools.partial(_hgrn2_kernel, n_chunks=n_chunks, n_heads=hp, side_cast=bool(rc)),
        grid=(B, H // hp, ns),
        in_specs=in_specs, out_specs=out_specs, out_shape=out_shape,
        scratch_shapes=[pltpu.VMEM((hp, HG_D, HG_D), F32)],
        compiler_params=_cparams("parallel", "parallel", "arbitrary"),
        name="hgrn2",
    )(*args)
    return res[0], (res[1] if rc else None)


def _mlstm_kernel(q_ref, k_ref, v_ref, og_ref, gt_ref, cw_ref, cb_ref, gb_ref, ng_ref, *rest,
                  n_chunks, side_cast):
    if side_cast:
        cast_in_ref, o_ref, cast_out_ref, cq_ref, ck_ref, c_ref, n_ref, m_ref = rest
        cast_out_ref[...] = cast_in_ref[...].astype(BF16)
    else:
        o_ref, cq_ref, ck_ref, c_ref, n_ref, m_ref = rest
    rows = n_chunks * CHUNK
    W = ML_HEADS * ML_DQK

    @pl.when(pl.program_id(1) == 0)
    def _():
        cq_ref[0:CONV_PAD, :] = jnp.zeros((CONV_PAD, W), F32)
        ck_ref[0:CONV_PAD, :] = jnp.zeros((CONV_PAD, W), F32)
        c_ref[...] = jnp.zeros_like(c_ref)
        n_ref[...] = jnp.zeros_like(n_ref)
        m_ref[...] = jnp.zeros_like(m_ref)

    cq_ref[CONV_PAD:CONV_PAD + rows, :] = q_ref[...]
    ck_ref[CONV_PAD:CONV_PAD + rows, :] = k_ref[...]

    row = lax.broadcasted_iota(jnp.int32, (CHUNK, CHUNK), 0)
    col = lax.broadcasted_iota(jnp.int32, (CHUNK, CHUNK), 1)
    causal = col <= row
    tril = causal.astype(F32)
    lane = lax.broadcasted_iota(jnp.int32, (CHUNK, LANES), 1)
    k_scale = ML_DQK ** -0.5

    for c in range(n_chunks):
        r0 = c * CHUNK
        pre = gt_ref[r0:r0 + CHUNK, :] + gb_ref[...]
        log_f = jnp.minimum(pre, 0.0) - jnp.log1p(jnp.exp(-jnp.abs(pre)))
        bcum = jnp.dot(tril, log_f, precision=HIGHEST, preferred_element_type=F32)
        x = jnp.where(lane < ML_HEADS, pre, bcum)
        xt = x.T

        for h in range(ML_HEADS):
            cs = slice(h * ML_DQK, (h + 1) * ML_DQK)
            vs = slice(h * ML_DV, (h + 1) * ML_DV)

            def conv(src_ref, w_off):
                acc = cb_ref[0:1, w_off + h * ML_DQK:w_off + (h + 1) * ML_DQK]
                for j in range(CONV_W):
                    start = CONV_PAD + r0 - (CONV_W - 1) + j
                    acc = acc + (cw_ref[j:j + 1, w_off + h * ML_DQK:w_off + (h + 1) * ML_DQK]
                                 * src_ref[start:start + CHUNK, cs])
                return acc * _sigmoid(acc)

            q = conv(cq_ref, 0)
            k = conv(ck_ref, W) * k_scale
            v = v_ref[r0:r0 + CHUNK, vs]
            q_bf = q.astype(BF16)
            v_bf = v.astype(BF16)

            li_c = x[:, h:h + 1]
            b_c = x[:, ML_HEADS + h:ML_HEADS + h + 1]
            li_r = xt[h:h + 1, :]
            b_r = xt[ML_HEADS + h:ML_HEADS + h + 1, :]
            b_last = b_c[CHUNK - 1:CHUNK, :]
            m_prev = m_ref[h][0:1, 0:1]
            c_state = c_ref[h]
            n_state = n_ref[h][0:1, :]

            log_intra = jnp.where(causal, b_c - b_r + li_r, NEG_BIG)
            log_inter = b_c + m_prev
            m_t = jnp.maximum(log_inter, jnp.max(log_intra, axis=-1, keepdims=True))
            w_intra = jnp.exp(log_intra - m_t)
            w_inter = jnp.exp(log_inter - m_t)
            scores = lax.dot_general(q_bf, k.astype(BF16), (((1,), (1,)), ((), ())),
                                     preferred_element_type=F32) * w_intra
            num = (jnp.dot(scores.astype(BF16), v_bf, preferred_element_type=F32)
                   + w_inter * jnp.dot(q_bf, c_state.astype(BF16), preferred_element_type=F32))
            den = (jnp.sum(scores, axis=-1, keepdims=True)
                   + w_inter * jnp.sum(q * n_state, axis=-1, keepdims=True))
            hh = num / jnp.maximum(jnp.abs(den), jnp.exp(-m_t))

            m_new = m_t[CHUNK - 1:CHUNK, :]
            w_write = jnp.exp(b_last - b_c + li_c - m_new)
            w_carry = jnp.exp(b_last + m_prev - m_new)
            kw = k * w_write
            c_ref[h] = w_carry * c_state + jnp.dot(kw.T.astype(BF16), v_bf,
                                                   preferred_element_type=F32)
            n_new = w_carry * n_state + jnp.sum(kw, axis=0, keepdims=True)
            n_ref[h] = jnp.broadcast_to(n_new, (SUBLANES, ML_DQK))
            m_ref[h] = jnp.broadcast_to(m_new, (SUBLANES, LANES))

            hn = hh * lax.rsqrt(jnp.mean(hh * hh, axis=-1, keepdims=True) + RMS_EPS) * ng_ref[0:1, vs]
            og = og_ref[r0:r0 + CHUNK, vs]
            o_ref[r0:r0 + CHUNK, vs] = (_sigmoid(og) * hn).astype(o_ref.dtype)

    cq_ref[0:CONV_PAD, :] = cq_ref[rows:rows + CONV_PAD, :]
    ck_ref[0:CONV_PAD, :] = ck_ref[rows:rows + CONV_PAD, :]


def _mlstm(proj, gates, conv_w, conv_b, gate_bias, ng, B, S, rows, col0, cast_src=None):
    T = B * S
    ns = S // rows
    n_chunks = rows // CHUNK
    W = ML_HEADS * ML_DQK
    V = ML_HEADS * ML_DV
    qb = col0 // W
    vb = (col0 + 2 * W) // V
    rc = _side_cast_rows(cast_src, B * ns)

    def rowmap(cb):
        return lambda b, j, cb=cb: (b * ns + j, cb)

    const = lambda b, j: (0, 0)
    in_specs = [pl.BlockSpec((rows, W), rowmap(qb)),
                pl.BlockSpec((rows, W), rowmap(qb + 1)),
                pl.BlockSpec((rows, V), rowmap(vb)),
                pl.BlockSpec((rows, V), rowmap(vb + 1)),
                pl.BlockSpec((rows, LANES), rowmap(0)),
                pl.BlockSpec((CONV_W, 2 * W), const),
                pl.BlockSpec((1, 2 * W), const),
                pl.BlockSpec((1, LANES), const),
                pl.BlockSpec((1, V), const)]
    out_specs = [pl.BlockSpec((rows, V), rowmap(0))]
    out_shape = [jax.ShapeDtypeStruct((T, V), BF16)]
    args = [proj, proj, proj, proj, gates, conv_w, conv_b, gate_bias, ng]
    if rc:
        w3d, w_layer = cast_src
        in_specs.append(pl.BlockSpec((None, rc, w3d.shape[2]), lambda b, j: (w_layer, b * ns + j, 0)))
        out_specs.append(pl.BlockSpec((rc, w3d.shape[2]), rowmap(0)))
        out_shape.append(jax.ShapeDtypeStruct(w3d.shape[1:], BF16))
        args.append(w3d)
    res = pl.pallas_call(
        functools.partial(_mlstm_kernel, n_chunks=n_chunks, side_cast=bool(rc)),
        grid=(B, ns),
        in_specs=in_specs, out_specs=out_specs, out_shape=out_shape,
        scratch_shapes=[pltpu.VMEM((rows + 2 * CONV_PAD, W), F32),
                        pltpu.VMEM((rows + 2 * CONV_PAD, W), F32),
                        pltpu.VMEM((ML_HEADS, ML_DQK, ML_DV), F32),
                        pltpu.VMEM((ML_HEADS, SUBLANES, ML_DQK), F32),
                        pltpu.VMEM((ML_HEADS, SUBLANES, LANES), F32)],
        compiler_params=_cparams("parallel", "arbitrary"),
        name="mlstm",
    )(*args)
    return res[0], (res[1] if rc else None)


def _top2(logits):
    lane = lax.broadcasted_iota(jnp.int32, logits.shape, 1)
    lg = jnp.where(lane < N_EXPERTS, logits, -jnp.inf)
    m0 = jnp.max(lg, axis=-1, keepdims=True)
    i0 = jnp.min(jnp.where(lg == m0, lane, LANES), axis=-1, keepdims=True)
    lg1 = jnp.where(lane == i0, -jnp.inf, lg)
    m1 = jnp.max(lg1, axis=-1, keepdims=True)
    i1 = jnp.min(jnp.where(lg1 == m1, lane, LANES), axis=-1, keepdims=True)
    e1 = jnp.exp(m1 - m0)
    g0 = 1.0 / (1.0 + e1)
    g1 = e1 / (1.0 + e1)
    idx = jnp.where(lane == 0, i0, jnp.where(lane == 1, i1, 0))
    gate = jnp.where(lane == 0, g0, jnp.where(lane == 1, g1, 0.0))
    return idx, gate


def _outproj_kernel(hg_ref, ml_ref, w_ref, x_ref, g_ref, b_ref, *rest, route):
    if route:
        wr_ref, o_ref, idx_ref, gate_ref = rest
    else:
        (o_ref,) = rest
    half = hg_ref.shape[1]
    tm = x_ref.shape[0]
    parts = [slice(0, tm // 2), slice(tm // 2, tm)]
    mixed = [jnp.dot(hg_ref[r, :], w_ref[0:half, :], preferred_element_type=F32)
             + jnp.dot(ml_ref[r, :], w_ref[half:, :], preferred_element_type=F32) for r in parts]
    y = jnp.concatenate([_layer_norm(ALPHA * x_ref[r, :] + m, g_ref[...], b_ref[...])
                         for r, m in zip(parts, mixed)], axis=0)
    o_ref[...] = y
    if route:
        y_hi = y.astype(BF16)
        y_lo = (y - y_hi.astype(F32)).astype(BF16)
        r_hi = jnp.dot(y_hi, wr_ref[...], preferred_element_type=F32)
        r_lo = jnp.dot(y_lo, wr_ref[:, 0:LANES], preferred_element_type=F32)
        logits = r_hi[:, 0:LANES] + r_hi[:, LANES:] + r_lo
        idx, gate = _top2(logits)
        idx_ref[...] = idx
        gate_ref[...] = gate


def _outproj(hg, ml, w, x, g, b, w_router, tm):
    T, D = x.shape
    half = hg.shape[1]
    route = w_router is not None
    row = lambda i: (i, 0)
    const = lambda i: (0, 0)
    in_specs = [pl.BlockSpec((tm, half), row), pl.BlockSpec((tm, half), row),
                pl.BlockSpec((D, D), const), pl.BlockSpec((tm, D), row),
                pl.BlockSpec((1, D), const), pl.BlockSpec((1, D), const)]
    out_specs = [pl.BlockSpec((tm, D), row)]
    out_shape = [jax.ShapeDtypeStruct((T, D), F32)]
    args = [hg, ml, w, x, g, b]
    if route:
        in_specs.append(pl.BlockSpec((D, 2 * LANES), const))
        out_specs += [pl.BlockSpec((tm, LANES), row), pl.BlockSpec((tm, LANES), row)]
        out_shape += [jax.ShapeDtypeStruct((T, LANES), jnp.int32),
                      jax.ShapeDtypeStruct((T, LANES), F32)]
        args.append(w_router)
    return pl.pallas_call(
        functools.partial(_outproj_kernel, route=route),
        grid=(T // tm,),
        in_specs=in_specs, out_specs=out_specs, out_shape=out_shape,
        compiler_params=_cparams("parallel"),
        name="out_proj_route" if route else "out_proj",
    )(*args)


def _ffn_kernel(x_ref, wg_ref, wu_ref, wd_ref, g_ref, b_ref, o_ref, xb_ref):
    f = pl.program_id(1)

    @pl.when(f == 0)
    def _():
        xb_ref[...] = x_ref[...].astype(BF16)
        o_ref[...] = jnp.zeros_like(o_ref)

    xb = xb_ref[...]
    gt = jnp.dot(xb, wg_ref[...], preferred_element_type=F32)
    up = jnp.dot(xb, wu_ref[...], preferred_element_type=F32)
    h = (gt * _sigmoid(gt) * up).astype(BF16)
    o_ref[...] += jnp.dot(h, wd_ref[...], preferred_element_type=F32)

    @pl.when(f == pl.num_programs(1) - 1)
    def _():
        o_ref[...] = _layer_norm(ALPHA * x_ref[...] + o_ref[...], g_ref[...], b_ref[...])


def _ffn(x, wg, wu, wd, g, b, tm, tf):
    T, D = x.shape
    Fp = wg.shape[1]
    return pl.pallas_call(
        _ffn_kernel,
        grid=(T // tm, Fp // tf),
        in_specs=[pl.BlockSpec((tm, D), lambda i, f: (i, 0)),
                  pl.BlockSpec((D, tf), lambda i, f: (0, f)),
                  pl.BlockSpec((D, tf), lambda i, f: (0, f)),
                  pl.BlockSpec((tf, D), lambda i, f: (f, 0)),
                  pl.BlockSpec((1, D), lambda i, f: (0, 0)),
                  pl.BlockSpec((1, D), lambda i, f: (0, 0))],
        out_specs=pl.BlockSpec((tm, D), lambda i, f: (i, 0)),
        out_shape=jax.ShapeDtypeStruct((T, D), F32),
        scratch_shapes=[pltpu.VMEM((tm, D), BF16)],
        compiler_params=_cparams("parallel", "arbitrary"),
        name="dense_ffn",
    )(x, wg, wu, wd, g, b)


def _gather_rows(src_hbm, idx_ref, dst_ref, sem, n, *, wait):
    if wait:
        pltpu.make_async_copy(src_hbm.at[pl.ds(0, n), :], dst_ref, sem).wait()
        return
    for r in range(n):
        pltpu.make_async_copy(src_hbm.at[pl.ds(idx_ref[0, 0, r], 1), :],
                              dst_ref.at[pl.ds(r, 1), :], sem).start()


def _moe_kernel(be_ref, nv_ref, tok_ref, tok_next_ref, x_hbm, wg_ref, wu_ref, wd_ref, o_ref,
                xg_ref, xb_ref, sem, *, blk):
    del be_ref
    i = pl.program_id(0)
    f = pl.program_id(1)
    n_valid = nv_ref[0]
    live = i < n_valid
    slot = i % 2

    @pl.when(f == 0)
    def _():
        o_ref[...] = jnp.zeros_like(o_ref)

    @pl.when((i == 0) & (f == 0))
    def _():
        _gather_rows(x_hbm, tok_ref, xg_ref.at[0], sem.at[0], blk, wait=False)

    @pl.when(live & (f == 0))
    def _():
        _gather_rows(x_hbm, tok_ref, xg_ref.at[slot], sem.at[slot], blk, wait=True)
        xb_ref[...] = xg_ref[slot].astype(BF16)

    @pl.when((i + 1 < n_valid) & (f == 0))
    def _():
        _gather_rows(x_hbm, tok_next_ref, xg_ref.at[1 - slot], sem.at[1 - slot], blk, wait=False)

    @pl.when(live)
    def _():
        xb = xb_ref[...]
        gt = jnp.dot(xb, wg_ref[...], preferred_element_type=F32)
        up = jnp.dot(xb, wu_ref[...], preferred_element_type=F32)
        h = (gt * _sigmoid(gt) * up).astype(BF16)
        o_ref[...] += jnp.dot(h, wd_ref[...], preferred_element_type=F32)


def _moe_ffn(block_expert, n_valid, slot_token, x, wg, wu, wd, blk, tf):
    T, D = x.shape
    F = wg.shape[2]
    n_blocks = block_expert.shape[0]
    nf = F // tf
    tok3 = slot_token.reshape(n_blocks, 1, blk)

    def f_eff(i, f, nv):
        return jnp.where(i < nv[0], f, nf - 1)

    grid_spec = pltpu.PrefetchScalarGridSpec(
        num_scalar_prefetch=2,
        grid=(n_blocks, nf),
        in_specs=[pl.BlockSpec((1, 1, blk), lambda i, f, be, nv: (i, 0, 0), memory_space=pltpu.SMEM),
                  pl.BlockSpec((1, 1, blk), lambda i, f, be, nv: (jnp.minimum(i + 1, n_blocks - 1), 0, 0),
                               memory_space=pltpu.SMEM),
                  pl.BlockSpec(memory_space=pl.ANY),
                  pl.BlockSpec((None, D, tf), lambda i, f, be, nv: (be[i], 0, f_eff(i, f, nv))),
                  pl.BlockSpec((None, D, tf), lambda i, f, be, nv: (be[i], 0, f_eff(i, f, nv))),
                  pl.BlockSpec((None, tf, D), lambda i, f, be, nv: (be[i], f_eff(i, f, nv), 0))],
        out_specs=pl.BlockSpec((blk, D), lambda i, f, be, nv: (i, 0)),
        scratch_shapes=[pltpu.VMEM((2, blk, D), F32), pltpu.VMEM((blk, D), BF16),
                        pltpu.SemaphoreType.DMA((2,))],
    )
    return pl.pallas_call(
        functools.partial(_moe_kernel, blk=blk),
        grid_spec=grid_spec,
        out_shape=jax.ShapeDtypeStruct((n_blocks * blk, D), F32),
        compiler_params=_cparams("arbitrary", "arbitrary"),
        name="moe_ffn",
    )(block_expert, n_valid, tok3, tok3, x, wg, wu, wd)


def _combine_kernel(dest_ref, dest_next_ref, ys_hbm, x_ref, gate_ref, g_ref, b_ref, o_ref,
                    buf_ref, sem, *, tm):
    i = pl.program_id(0)
    slot = i % 2
    n = TOP_K * tm

    @pl.when(i == 0)
    def _():
        _gather_rows(ys_hbm, dest_ref, buf_ref.at[0], sem.at[0], n, wait=False)

    @pl.when(i + 1 < pl.num_programs(0))
    def _():
        _gather_rows(ys_hbm, dest_next_ref, buf_ref.at[1 - slot], sem.at[1 - slot], n, wait=False)

    _gather_rows(ys_hbm, dest_ref, buf_ref.at[slot], sem.at[slot], n, wait=True)
    gate = gate_ref[...]
    ff = buf_ref[slot, 0:tm, :] * gate[:, 0:1] + buf_ref[slot, tm:2 * tm, :] * gate[:, 1:2]
    o_ref[...] = _layer_norm(ALPHA * x_ref[...] + ff, g_ref[...], b_ref[...])


def _combine(dest, ys, x, gate, g, b, tm):
    T, D = x.shape
    nt = T // tm
    dest3 = dest.reshape(nt, tm, TOP_K).transpose(0, 2, 1).reshape(nt, 1, TOP_K * tm)
    return pl.pallas_call(
        functools.partial(_combine_kernel, tm=tm),
        grid=(nt,),
        in_specs=[pl.BlockSpec((1, 1, TOP_K * tm), lambda i: (i, 0, 0), memory_space=pltpu.SMEM),
                  pl.BlockSpec((1, 1, TOP_K * tm), lambda i: (jnp.minimum(i + 1, nt - 1), 0, 0),
                               memory_space=pltpu.SMEM),
                  pl.BlockSpec(memory_space=pl.ANY),
                  pl.BlockSpec((tm, D), lambda i: (i, 0)),
                  pl.BlockSpec((tm, LANES), lambda i: (i, 0)),
                  pl.BlockSpec((1, D), lambda i: (0, 0)),
                  pl.BlockSpec((1, D), lambda i: (0, 0))],
        out_specs=pl.BlockSpec((tm, D), lambda i: (i, 0)),
        out_shape=jax.ShapeDtypeStruct((T, D), F32),
        scratch_shapes=[pltpu.VMEM((2, TOP_K * tm, D), F32), pltpu.SemaphoreType.DMA((2,))],
        compiler_params=_cparams("arbitrary"),
        name="moe_combine",
    )(dest3, dest3, ys, x, gate, g, b)


def _dispatch(idx2, blk, n_blocks):
    T = idx2.shape[0]
    n_assign = T * TOP_K
    expert = idx2.reshape(-1)
    onehot = (expert[:, None] == jnp.arange(N_EXPERTS, dtype=jnp.int32)[None, :]).astype(jnp.int32)
    csum = jnp.cumsum(onehot, axis=0)
    rank = jnp.sum(onehot * (csum - 1), axis=1)
    counts = csum[-1]
    padded = (counts + blk - 1) // blk * blk
    padded_end = jnp.cumsum(padded)
    dest = (padded_end - padded)[expert] + rank
    token = jnp.arange(n_assign, dtype=jnp.int32) // TOP_K
    slot_token = jnp.zeros((n_blocks * blk,), jnp.int32).at[dest].set(token)
    block_expert = jnp.minimum(
        jnp.searchsorted(padded_end, jnp.arange(n_blocks, dtype=jnp.int32) * blk, side="right"),
        N_EXPERTS - 1).astype(jnp.int32)
    n_valid = (padded_end[-1:] // blk).astype(jnp.int32)
    block_ids = jnp.arange(n_blocks, dtype=jnp.int32)
    block_expert = jnp.where(block_ids < n_valid[0], block_expert,
                             block_expert[jnp.maximum(n_valid[0] - 1, 0)])
    return dest.reshape(T, TOP_K).astype(jnp.int32), slot_token, block_expert, n_valid


def _tile(n, pref):
    t = min(n, pref)
    while n % t:
        t //= 2
    return t


def kernel(x, w_in, w_out, hg_lb_logits, hg_norm_g, ml_conv_w, ml_conv_b, ml_b_i, ml_b_f, ml_norm_g,
           ln_mix_g, ln_mix_b, ln_ffn_g, ln_ffn_b, ffn_w_gate, ffn_w_up, ffn_w_down,
           moe_w_router, moe_w_gate, moe_w_up, moe_w_down):
    B, S, D = x.shape
    T = B * S
    depth = w_in.shape[0]
    hg_w = HG_HEADS * HG_D
    main_cols = 4 * hg_w + 2 * ML_HEADS * ML_DQK + 2 * ML_HEADS * ML_DV

    p = jax.nn.softmax(hg_lb_logits.astype(F32), axis=0)
    lower_bounds = jnp.cumsum(p, axis=0) - p[0]

    rows = _tile(S, 256)
    tm_proj = _tile(T, 1024)
    tm = _tile(T, 512)
    blk = _tile(T, 512)
    n_blocks = -(-(T * TOP_K + N_EXPERTS * (blk - 1)) // blk)
    wt = 1024

    f_dense = ffn_w_gate.shape[2]
    f_moe = moe_w_gate.shape[3]
    tf_dense = 512
    f_dense_p = -(-f_dense // tf_dense) * tf_dense

    w_gates_all = jnp.pad(lax.slice_in_dim(w_in, main_cols, w_in.shape[2], axis=2),
                          ((0, 0), (0, 0), (0, LANES - 2 * ML_HEADS))).astype(BF16)

    n_layers_moe = moe_w_gate.shape[0]
    moe_f32 = {"gate": moe_w_gate.reshape(n_layers_moe, N_EXPERTS * D, f_moe),
               "up": moe_w_up.reshape(n_layers_moe, N_EXPERTS * D, f_moe),
               "down": moe_w_down.reshape(n_layers_moe, N_EXPERTS * f_moe, D)}
    moe_bf16 = {}

    def moe_weight(name, j):
        w = moe_bf16.get((name, j))
        if w is None:
            _, R, C = moe_f32[name].shape
            w = _cast_bf16(moe_f32[name], j, R, C, wt, wt)
        return w.reshape((N_EXPERTS, D, f_moe) if name != "down" else (N_EXPERTS, f_moe, D))

    xt = x.reshape(T, D).astype(F32)
    for layer in range(depth):
        moe = layer % 2 == 1
        j = layer // 2
        if moe:
            hg_cast, ml_cast = ("down", j), None
        elif layer + 1 < depth:
            hg_cast, ml_cast = ("gate", j), ("up", j)
        else:
            hg_cast, ml_cast = None, None

        w_main = w_in[layer, :, :main_cols].astype(BF16)
        proj, gates = _proj(xt, w_main, w_gates_all[layer], tm_proj, main_cols // 4)

        hg, side = _hgrn2(proj, lower_bounds[layer][None, :], hg_norm_g[layer][None, :].astype(F32), B, S,
                          _tile(S, 512), HG_HEADS // 2, cast_src=(moe_f32[hg_cast[0]], hg_cast[1]) if hg_cast else None)
        if side is not None:
            moe_bf16[hg_cast] = side
        gate_bias = jnp.pad(jnp.concatenate([ml_b_i[layer], ml_b_f[layer]]).astype(F32),
                            (0, LANES - 2 * ML_HEADS))[None, :]
        ml, side = _mlstm(proj, gates, ml_conv_w[layer].astype(F32), ml_conv_b[layer][None, :].astype(F32),
                          gate_bias, ml_norm_g[layer][None, :].astype(F32), B, S, rows, 4 * hg_w,
                          cast_src=(moe_f32[ml_cast[0]], ml_cast[1]) if ml_cast else None)
        if side is not None:
            moe_bf16[ml_cast] = side

        w_router = None
        if moe:
            wr = jnp.pad(moe_w_router[j].astype(F32), ((0, 0), (0, LANES - N_EXPERTS)))
            wr_hi = wr.astype(BF16)
            w_router = jnp.concatenate([wr_hi, (wr - wr_hi.astype(F32)).astype(BF16)], axis=1)
        res = _outproj(hg, ml, _cast_bf16(w_out, layer, D, D, wt, wt), xt,
                       ln_mix_g[layer][None, :], ln_mix_b[layer][None, :], w_router, tm)
        g2 = ln_ffn_g[layer][None, :]
        b2 = ln_ffn_b[layer][None, :]
        if not moe:
            (x1,) = res
            wg = _cast_bf16(ffn_w_gate, j, D, f_dense_p, wt, tf_dense)
            wu = _cast_bf16(ffn_w_up, j, D, f_dense_p, wt, tf_dense)
            wd = _cast_bf16(ffn_w_down, j, f_dense_p, D, tf_dense, wt)
            xt = _ffn(x1, wg, wu, wd, g2, b2, tm, tf_dense)
        else:
            x1, idx, gate = res
            dest, slot_token, block_expert, n_valid = _dispatch(idx[:, :TOP_K], blk, n_blocks)
            ys = _moe_ffn(block_expert, n_valid, slot_token, x1, moe_weight("gate", j), moe_weight("up", j),
                          moe_weight("down", j), blk, 1024)
            xt = _combine(dest, ys, x1, gate, g2, b2, _tile(T, 256))
    return xt.reshape(B, S, D).astype(x.dtype)
```

```python
import functools

import jax
import jax.numpy as jnp
import numpy as np
from jax import lax
from jax.experimental import pallas as pl
from jax.experimental.pallas import tpu as pltpu

F32 = jnp.float32
BF16 = jnp.bfloat16
HIGHEST = lax.Precision.HIGHEST

MODEL_DEPTH = 4
HG_HEADS = 8
HG_D = 128
ML_HEADS = 4
ML_DQK = 128
ML_DV = 256
CONV_W = 4
N_EXPERTS = 8
TOP_K = 2
ALPHA = (2 * MODEL_DEPTH) ** 0.25
LN_EPS = 1e-5
RMS_EPS = 1e-6
NEG_BIG = -1e30
F_MIN = 1e-30

LANES = 128
SUBLANES = 8
VMEM_LIMIT_BYTES = 56 * 1024 * 1024
SIDE_CAST_BLOCK_BYTES = 4 * 1024 * 1024

CHUNK = 64
CONV_PAD = SUBLANES


def _cparams(*sem):
    return pltpu.CompilerParams(dimension_semantics=sem, vmem_limit_bytes=VMEM_LIMIT_BYTES)


def _sigmoid(x):
    return 1.0 / (1.0 + jnp.exp(-x))


def _layer_norm(y, g, b):
    mu = jnp.mean(y, axis=-1, keepdims=True)
    yc = y - mu
    var = jnp.mean(yc * yc, axis=-1, keepdims=True)
    return yc * lax.rsqrt(var + LN_EPS) * g + b


def _cast_kernel(x_ref, o_ref, *, rows_in, cols_in, mask):
    x = x_ref[...]
    if mask:
        tr, tc = x.shape
        r = pl.program_id(0) * tr + lax.broadcasted_iota(jnp.int32, x.shape, 0)
        c = pl.program_id(1) * tc + lax.broadcasted_iota(jnp.int32, x.shape, 1)
        x = jnp.where((r < rows_in) & (c < cols_in), x, 0.0)
    o_ref[...] = x.astype(BF16)


def _cast_bf16(w, layer, rows_out, cols_out, tr, tc):
    _, R, C = w.shape
    mask = rows_out > R or cols_out > C
    return pl.pallas_call(
        functools.partial(_cast_kernel, rows_in=R, cols_in=C, mask=mask),
        grid=(rows_out // tr, cols_out // tc),
        in_specs=[pl.BlockSpec((None, tr, tc), lambda i, j: (layer, i, j))],
        out_specs=pl.BlockSpec((tr, tc), lambda i, j: (i, j)),
        out_shape=jax.ShapeDtypeStruct((rows_out, cols_out), BF16),
        compiler_params=_cparams("parallel", "parallel"),
        name="cast_bf16",
    )(w)


def _proj_kernel(x_ref, w_ref, wgate_ref, o_ref, ogate_ref):
    xb = x_ref[...].astype(BF16)
    o_ref[...] = jnp.dot(xb, w_ref[...], preferred_element_type=F32)

    @pl.when(pl.program_id(1) == 0)
    def _():
        ogate_ref[...] = jnp.dot(xb, wgate_ref[...], preferred_element_type=F32)


def _proj(x, w, w_gates, tm, tn):
    T, D = x.shape
    N = w.shape[1]
    return pl.pallas_call(
        _proj_kernel,
        grid=(T // tm, N // tn),
        in_specs=[pl.BlockSpec((tm, D), lambda i, j: (i, 0)),
                  pl.BlockSpec((D, tn), lambda i, j: (0, j)),
                  pl.BlockSpec((D, LANES), lambda i, j: (0, 0))],
        out_specs=[pl.BlockSpec((tm, tn), lambda i, j: (i, j)),
                   pl.BlockSpec((tm, LANES), lambda i, j: (i, 0))],
        out_shape=[jax.ShapeDtypeStruct((T, N), F32), jax.ShapeDtypeStruct((T, LANES), F32)],
        compiler_params=_cparams("parallel", "arbitrary"),
        name="in_proj",
    )(x, w, w_gates)


_HG_LEVELS = (32, 16, 8, 4, 2, 1)
_HG_DIAG = len(_HG_LEVELS)


def _hgrn2_tables():
    t = np.arange(CHUNK)[:, None]
    s = np.arange(CHUNK)[None, :]
    level = np.full((CHUNK, CHUNK), -1, np.int32)
    level[t == s] = _HG_DIAG
    tril = (s <= t).astype(np.float32)
    mats = [tril]
    for li, m in enumerate(_HG_LEVELS):
        mask = (t // (2 * m) == s // (2 * m)) & (t % (2 * m) >= m) & (s % (2 * m) < m)
        assert not (level[mask] >= 0).any()
        level[mask] = li
        if m < SUBLANES:
            mid = (t // (2 * m)) * (2 * m) + m
            mats.append(tril - (s <= mid - 1).astype(np.float32))
    assert (level[s < t] >= 0).all() and (level[s > t] == -1).all()
    return np.concatenate(mats, axis=0), level


def _mix_kernel(hq_ref, hf_ref, hi_ref, hg_ref, lb_ref, hng_ref, w_ref, lv_ref,
                mq_ref, mk_ref, mv_ref, mog_ref, gt_ref, cw_ref, cb_ref, gb_ref, mng_ref, *rest,
                n_chunks, n_side):
    side_in = rest[:n_side]
    ho_ref, mo_ref = rest[n_side:n_side + 2]
    side_out = rest[n_side + 2:2 * n_side + 2]
    st_ref, cq_ref, ck_ref, c_ref, n_ref, m_ref = rest[2 * n_side + 2:]
    for src_ref, dst_ref in zip(side_in, side_out):
        dst_ref[...] = src_ref[...].astype(BF16)

    rows = n_chunks * CHUNK
    W = ML_HEADS * ML_DQK
    nt = (((1,), (1,)), ((), ()))

    @pl.when(pl.program_id(1) == 0)
    def _():
        st_ref[...] = jnp.zeros_like(st_ref)
        cq_ref[0:CONV_PAD, :] = jnp.zeros((CONV_PAD, W), F32)
        ck_ref[0:CONV_PAD, :] = jnp.zeros((CONV_PAD, W), F32)
        c_ref[...] = jnp.zeros_like(c_ref)
        n_ref[...] = jnp.zeros_like(n_ref)
        m_ref[...] = jnp.zeros_like(m_ref)

    cq_ref[CONV_PAD:CONV_PAD + rows, :] = mq_ref[...]
    ck_ref[CONV_PAD:CONV_PAD + rows, :] = mk_ref[...]

    w = w_ref[...]
    lv = lv_ref[...]
    ones_bf = jnp.ones((HG_D, CHUNK), BF16)
    units = [(hh, c) for hh in range(HG_HEADS) for c in range(n_chunks)]
    rows_of = {u: slice(u[1] * CHUNK, (u[1] + 1) * CHUNK) for u in units}
    lanes_of = {u: slice(u[0] * HG_D, (u[0] + 1) * HG_D) for u in units}
    q, k, v_bf, wb, p, qd, upd, dec, o = {}, {}, {}, {}, {}, {}, {}, {}, {}

    def h_phase1(u):
        lb = lb_ref[:, lanes_of[u]]
        one_m_lb = 1.0 - lb
        qp = hq_ref[rows_of[u], lanes_of[u]]
        fp = hf_ref[rows_of[u], lanes_of[u]]
        q[u] = qp * _sigmoid(qp)
        e_f = jnp.exp(-fp)
        sig_f = 1.0 / (1.0 + e_f)
        log_f = jnp.log2(jnp.maximum(lb + one_m_lb * sig_f, F_MIN))
        k[u] = one_m_lb * (e_f * sig_f)
        v_bf[u] = hi_ref[rows_of[u], lanes_of[u]].astype(BF16)
        lf_hi = log_f.astype(BF16)
        lf_lo = (log_f - lf_hi.astype(F32)).astype(BF16)
        wb[u] = (jnp.dot(w, lf_hi, preferred_element_type=F32)
                 + jnp.dot(w, lf_lo, preferred_element_type=F32))

    def h_phase2(u):
        b = wb[u][0:CHUNK, :]
        b_last = b[CHUNK - 1:CHUNK, :]
        qd[u] = (q[u] * jnp.exp2(b)).astype(BF16)
        kd = k[u] * jnp.exp2(b_last - b)
        upd[u] = jnp.dot(hi_ref[rows_of[u], lanes_of[u]].T.astype(BF16), kd.astype(BF16),
                         preferred_element_type=F32)
        dec[u] = jnp.exp2(b_last)
        pc = jnp.where(lv == _HG_DIAG,
                       jnp.dot((q[u] * k[u]).astype(BF16), ones_bf, preferred_element_type=F32), 0.0)
        small = 0
        for li, m in enumerate(_HG_LEVELS):
            if m >= SUBLANES:
                zero = jnp.zeros((m, HG_D), F32)
                qs, ks = [], []
                for lo in range(0, CHUNK, 2 * m):
                    mid = lo + m
                    ref = b[mid - 1:mid, :]
                    ks += [k[u][lo:mid, :] * jnp.exp2(ref - b[lo:mid, :]), zero]
                    qs += [zero, q[u][mid:mid + m, :] * jnp.exp2(b[mid:mid + m, :] - ref)]
                qt = jnp.concatenate(qs, axis=0)
                kt = jnp.concatenate(ks, axis=0)
            else:
                small += 1
                d = wb[u][small * CHUNK:(small + 1) * CHUNK, :]
                qt = q[u] * jnp.exp2(jnp.minimum(d, 0.0))
                kt = k[u] * jnp.exp2(jnp.minimum(-d, 0.0))
            s_l = lax.dot_general(qt.astype(BF16), kt.astype(BF16), nt, preferred_element_type=F32)
            pc = jnp.where(lv == li, s_l, pc)
        p[u] = pc.astype(BF16)

    def h_phase3(hh):
        st = st_ref[hh]
        for c in range(n_chunks):
            u = (hh, c)
            o[u] = lax.dot_general(qd[u], st.astype(BF16), nt, preferred_element_type=F32)
            st = st * dec[u] + upd[u]
        st_ref[hh] = st

    def h_phase4(u):
        oc = o[u] + jnp.dot(p[u], v_bf[u], preferred_element_type=F32)
        gp = hg_ref[rows_of[u], lanes_of[u]]
        on = oc * lax.rsqrt(jnp.mean(oc * oc, axis=-1, keepdims=True) + RMS_EPS) * hng_ref[:, lanes_of[u]]
        ho_ref[rows_of[u], lanes_of[u]] = (on * (gp * _sigmoid(gp))).astype(ho_ref.dtype)

    row = lax.broadcasted_iota(jnp.int32, (CHUNK, CHUNK), 0)
    col = lax.broadcasted_iota(jnp.int32, (CHUNK, CHUNK), 1)
    causal = col <= row
    tril = causal.astype(F32)
    lane = lax.broadcasted_iota(jnp.int32, (CHUNK, LANES), 1)
    k_scale = ML_DQK ** -0.5
    xs, xts, mst = {}, {}, {}

    def m_gates(c):
        pre = gt_ref[c * CHUNK:(c + 1) * CHUNK, :] + gb_ref[...]
        log_f = jnp.minimum(pre, 0.0) - jnp.log1p(jnp.exp(-jnp.abs(pre)))
        bcum = jnp.dot(tril, log_f, precision=HIGHEST, preferred_element_type=F32)
        xs[c] = jnp.where(lane < ML_HEADS, pre, bcum)
        xts[c] = xs[c].T

    def m_first(c, h):
        r0 = c * CHUNK
        cs = slice(h * ML_DQK, (h + 1) * ML_DQK)
        vs = slice(h * ML_DV, (h + 1) * ML_DV)

        def conv(src_ref, w_off):
            acc = cb_ref[0:1, w_off + h * ML_DQK:w_off + (h + 1) * ML_DQK]
            for j in range(CONV_W):
                start = CONV_PAD + r0 - (CONV_W - 1) + j
                acc = acc + (cw_ref[j:j + 1, w_off + h * ML_DQK:w_off + (h + 1) * ML_DQK]
                             * src_ref[start:start + CHUNK, cs])
            return acc * _sigmoid(acc)

        qq = conv(cq_ref, 0)
        kk = conv(ck_ref, W) * k_scale
        v_b = mv_ref[r0:r0 + CHUNK, vs].astype(BF16)
        q_b = qq.astype(BF16)
        x, xt = xs[c], xts[c]
        li_c = x[:, h:h + 1]
        b_c = x[:, ML_HEADS + h:ML_HEADS + h + 1]
        li_r = xt[h:h + 1, :]
        b_r = xt[ML_HEADS + h:ML_HEADS + h + 1, :]
        log_intra = jnp.where(causal, b_c - b_r + li_r, NEG_BIG)
        mst[c, h] = dict(
            q=qq, k=kk, v_b=v_b, li_c=li_c, b_c=b_c, log_intra=log_intra,
            rmax=jnp.max(log_intra, axis=-1, keepdims=True),
            qk=lax.dot_general(q_b, kk.astype(BF16), nt, preferred_element_type=F32),
            qc=jnp.dot(q_b, c_ref[h].astype(BF16), preferred_element_type=F32),
            qn=jnp.sum(qq * n_ref[h][0:1, :], axis=-1, keepdims=True))

    def m_second(c, h):
        r0 = c * CHUNK
        vs = slice(h * ML_DV, (h + 1) * ML_DV)
        d = mst.pop((c, h))
        b_c, li_c = d["b_c"], d["li_c"]
        b_last = b_c[CHUNK - 1:CHUNK, :]
        m_prev = m_ref[h][0:1, 0:1]
        log_inter = b_c + m_prev
        m_t = jnp.maximum(log_inter, d["rmax"])
        w_inter = jnp.exp(log_inter - m_t)
        scores = d["qk"] * jnp.exp(d["log_intra"] - m_t)
        num = jnp.dot(scores.astype(BF16), d["v_b"], preferred_element_type=F32) + w_inter * d["qc"]
        den = jnp.sum(scores, axis=-1, keepdims=True) + w_inter * d["qn"]
        hh = num / jnp.maximum(jnp.abs(den), jnp.exp(-m_t))
        m_new = m_t[CHUNK - 1:CHUNK, :]
        kw = d["k"] * jnp.exp(b_last - b_c + li_c - m_new)
        w_carry = jnp.exp(b_last + m_prev - m_new)
        c_ref[h] = w_carry * c_ref[h] + jnp.dot(kw.T.astype(BF16), d["v_b"], preferred_element_type=F32)
        n_new = w_carry * n_ref[h][0:1, :] + jnp.sum(kw, axis=0, keepdims=True)
        n_ref[h] = jnp.broadcast_to(n_new, (SUBLANES, ML_DQK))
        m_ref[h] = jnp.broadcast_to(m_new, (SUBLANES, LANES))
        hn = hh * lax.rsqrt(jnp.mean(hh * hh, axis=-1, keepdims=True) + RMS_EPS) * mng_ref[0:1, vs]
        og = mog_ref[r0:r0 + CHUNK, vs]
        mo_ref[r0:r0 + CHUNK, vs] = (_sigmoid(og) * hn).astype(mo_ref.dtype)

    pairs = [(c, h) for c in range(n_chunks) for h in range(ML_HEADS)]
    for c in range(n_chunks):
        m_gates(c)
    h_pieces = ([functools.partial(h_phase1, u) for u in units]
                + [functools.partial(h_phase2, u) for u in units])
    m_pieces = []
    for i in range(len(pairs) + 1):
        if i < len(pairs):
            m_pieces.append(functools.partial(m_first, *pairs[i]))
        if i >= 1:
            m_pieces.append(functools.partial(m_second, *pairs[i - 1]))
    per = len(h_pieces) // len(m_pieces)
    for i, m_piece in enumerate(m_pieces):
        m_piece()
        for h_piece in h_pieces[i * per:(i + 1) * per]:
            h_piece()
    for h_piece in h_pieces[len(m_pieces) * per:]:
        h_piece()
    for hh in range(HG_HEADS):
        h_phase3(hh)
    for u in units:
        h_phase4(u)
    cq_ref[0:CONV_PAD, :] = cq_ref[rows:rows + CONV_PAD, :]
    ck_ref[0:CONV_PAD, :] = ck_ref[rows:rows + CONV_PAD, :]


def _side_cast_rows(cast_src, n_steps):
    _, R, C = cast_src[0].shape
    rc = R // n_steps
    ok = R % n_steps == 0 and rc % 16 == 0 and C % LANES == 0 and rc * C * 4 <= SIDE_CAST_BLOCK_BYTES
    return rc if ok else 0


def _mix(proj, gates, lb, hg_ng, conv_w, conv_b, gate_bias, ml_ng, B, S, rows, col0, cast_srcs=()):
    T = B * S
    ns = S // rows
    n_chunks = rows // CHUNK
    wd = HG_HEADS * HG_D
    w_np, level_np = _hgrn2_tables()
    w = jnp.asarray(w_np, BF16)
    level = jnp.asarray(level_np, jnp.int32)
    W = ML_HEADS * ML_DQK
    V = ML_HEADS * ML_DV
    qb = col0 // W
    vb = (col0 + 2 * W) // V
    riding = [(i, src, _side_cast_rows(src, B * ns)) for i, src in enumerate(cast_srcs)]
    riding = [(i, src, rc) for i, src, rc in riding if rc]

    def rowmap(cb):
        return lambda b, j, cb=cb: (b * ns + j, cb)

    const = lambda b, j: (0, 0)
    in_specs = [pl.BlockSpec((rows, wd), rowmap(0)), pl.BlockSpec((rows, wd), rowmap(1)),
                pl.BlockSpec((rows, wd), rowmap(2)), pl.BlockSpec((rows, wd), rowmap(3)),
                pl.BlockSpec((1, wd), const), pl.BlockSpec((1, wd), const),
                pl.BlockSpec(w.shape, const), pl.BlockSpec(level.shape, const),
                pl.BlockSpec((rows, W), rowmap(qb)), pl.BlockSpec((rows, W), rowmap(qb + 1)),
                pl.BlockSpec((rows, V), rowmap(vb)), pl.BlockSpec((rows, V), rowmap(vb + 1)),
                pl.BlockSpec((rows, LANES), rowmap(0)),
                pl.BlockSpec((CONV_W, 2 * W), const), pl.BlockSpec((1, 2 * W), const),
                pl.BlockSpec((1, LANES), const), pl.BlockSpec((1, V), const)]
    out_specs = [pl.BlockSpec((rows, wd), rowmap(0)), pl.BlockSpec((rows, V), rowmap(0))]
    out_shape = [jax.ShapeDtypeStruct((T, wd), BF16), jax.ShapeDtypeStruct((T, V), BF16)]
    args = [proj, proj, proj, proj, lb, hg_ng, w, level, proj, proj, proj, proj, gates,
            conv_w, conv_b, gate_bias, ml_ng]
    for _, (w3d, w_layer), rc in riding:
        in_specs.append(pl.BlockSpec((None, rc, w3d.shape[2]), lambda b, j, wl=w_layer: (wl, b * ns + j, 0)))
        out_specs.append(pl.BlockSpec((rc, w3d.shape[2]), rowmap(0)))
        out_shape.append(jax.ShapeDtypeStruct(w3d.shape[1:], BF16))
        args.append(w3d)
    res = pl.pallas_call(
        functools.partial(_mix_kernel, n_chunks=n_chunks, n_side=len(riding)),
        grid=(B, ns),
        in_specs=in_specs, out_specs=out_specs, out_shape=out_shape,
        scratch_shapes=[pltpu.VMEM((HG_HEADS, HG_D, HG_D), F32),
                        pltpu.VMEM((rows + 2 * CONV_PAD, W), F32),
                        pltpu.VMEM((rows + 2 * CONV_PAD, W), F32),
                        pltpu.VMEM((ML_HEADS, ML_DQK, ML_DV), F32),
                        pltpu.VMEM((ML_HEADS, SUBLANES, ML_DQK), F32),
                        pltpu.VMEM((ML_HEADS, SUBLANES, LANES), F32)],
        compiler_params=_cparams("parallel", "arbitrary"),
        name="mixers",
    )(*args)
    casts = [None] * len(cast_srcs)
    for (i, _, _), out in zip(riding, res[2:]):
        casts[i] = out
    return res[0], res[1], casts


def _top2(logits):
    lane = lax.broadcasted_iota(jnp.int32, logits.shape, 1)
    lg = jnp.where(lane < N_EXPERTS, logits, -jnp.inf)
    m0 = jnp.max(lg, axis=-1, keepdims=True)
    i0 = jnp.min(jnp.where(lg == m0, lane, LANES), axis=-1, keepdims=True)
    lg1 = jnp.where(lane == i0, -jnp.inf, lg)
    m1 = jnp.max(lg1, axis=-1, keepdims=True)
    i1 = jnp.min(jnp.where(lg1 == m1, lane, LANES), axis=-1, keepdims=True)
    e1 = jnp.exp(m1 - m0)
    g0 = 1.0 / (1.0 + e1)
    g1 = e1 / (1.0 + e1)
    idx = jnp.where(lane == 0, i0, jnp.where(lane == 1, i1, 0))
    gate = jnp.where(lane == 0, g0, jnp.where(lane == 1, g1, 0.0))
    return idx, gate


def _outproj_kernel(hg_ref, ml_ref, w_ref, x_ref, g_ref, b_ref, *rest, route):
    if route:
        wr_ref, o_ref, idx_ref, gate_ref = rest
    else:
        (o_ref,) = rest
    half = hg_ref.shape[1]
    tm = x_ref.shape[0]
    parts = [slice(0, tm // 2), slice(tm // 2, tm)]
    mixed = [jnp.dot(hg_ref[r, :], w_ref[0:half, :], preferred_element_type=F32)
             + jnp.dot(ml_ref[r, :], w_ref[half:, :], preferred_element_type=F32) for r in parts]
    y = jnp.concatenate([_layer_norm(ALPHA * x_ref[r, :] + m, g_ref[...], b_ref[...])
                         for r, m in zip(parts, mixed)], axis=0)
    o_ref[...] = y
    if route:
        y_hi = y.astype(BF16)
        y_lo = (y - y_hi.astype(F32)).astype(BF16)
        r_hi = jnp.dot(y_hi, wr_ref[...], preferred_element_type=F32)
        r_lo = jnp.dot(y_lo, wr_ref[:, 0:LANES], preferred_element_type=F32)
        logits = r_hi[:, 0:LANES] + r_hi[:, LANES:] + r_lo
        idx, gate = _top2(logits)
        idx_ref[...] = idx
        gate_ref[...] = gate


def _outproj(hg, ml, w, x, g, b, w_router, tm):
    T, D = x.shape
    half = hg.shape[1]
    route = w_router is not None
    row = lambda i: (i, 0)
    const = lambda i: (0, 0)
    in_specs = [pl.BlockSpec((tm, half), row), pl.BlockSpec((tm, half), row),
                pl.BlockSpec((D, D), const), pl.BlockSpec((tm, D), row),
                pl.BlockSpec((1, D), const), pl.BlockSpec((1, D), const)]
    out_specs = [pl.BlockSpec((tm, D), row)]
    out_shape = [jax.ShapeDtypeStruct((T, D), F32)]
    args = [hg, ml, w, x, g, b]
    if route:
        in_specs.append(pl.BlockSpec((D, 2 * LANES), const))
        out_specs += [pl.BlockSpec((tm, LANES), row), pl.BlockSpec((tm, LANES), row)]
        out_shape += [jax.ShapeDtypeStruct((T, LANES), jnp.int32),
                      jax.ShapeDtypeStruct((T, LANES), F32)]
        args.append(w_router)
    return pl.pallas_call(
        functools.partial(_outproj_kernel, route=route),
        grid=(T // tm,),
        in_specs=in_specs, out_specs=out_specs, out_shape=out_shape,
        compiler_params=_cparams("parallel"),
        name="out_proj_route" if route else "out_proj",
    )(*args)


def _ffn_kernel(x_ref, wg_ref, wu_ref, wd_ref, g_ref, b_ref, o_ref, xb_ref):
    f = pl.program_id(1)

    @pl.when(f == 0)
    def _():
        xb_ref[...] = x_ref[...].astype(BF16)
        o_ref[...] = jnp.zeros_like(o_ref)

    xb = xb_ref[...]
    gt = jnp.dot(xb, wg_ref[...], preferred_element_type=F32)
    up = jnp.dot(xb, wu_ref[...], preferred_element_type=F32)
    h = (gt * _sigmoid(gt) * up).astype(BF16)
    o_ref[...] += jnp.dot(h, wd_ref[...], preferred_element_type=F32)

    @pl.when(f == pl.num_programs(1) - 1)
    def _():
        o_ref[...] = _layer_norm(ALPHA * x_ref[...] + o_ref[...], g_ref[...], b_ref[...])


def _ffn(x, wg, wu, wd, g, b, tm, tf):
    T, D = x.shape
    Fp = wg.shape[1]
    return pl.pallas_call(
        _ffn_kernel,
        grid=(T // tm, Fp // tf),
        in_specs=[pl.BlockSpec((tm, D), lambda i, f: (i, 0)),
                  pl.BlockSpec((D, tf), lambda i, f: (0, f)),
                  pl.BlockSpec((D, tf), lambda i, f: (0, f)),
                  pl.BlockSpec((tf, D), lambda i, f: (f, 0)),
                  pl.BlockSpec((1, D), lambda i, f: (0, 0)),
                  pl.BlockSpec((1, D), lambda i, f: (0, 0))],
        out_specs=pl.BlockSpec((tm, D), lambda i, f: (i, 0)),
        out_shape=jax.ShapeDtypeStruct((T, D), F32),
        scratch_shapes=[pltpu.VMEM((tm, D), BF16)],
        compiler_params=_cparams("parallel", "arbitrary"),
        name="dense_ffn",
    )(x, wg, wu, wd, g, b)


def _gather_rows(src_hbm, idx_ref, dst_ref, sem, n, *, wait):
    if wait:
        pltpu.make_async_copy(src_hbm.at[pl.ds(0, n), :], dst_ref, sem).wait()
        return
    for r in range(n):
        pltpu.make_async_copy(src_hbm.at[pl.ds(idx_ref[0, 0, r], 1), :],
                              dst_ref.at[pl.ds(r, 1), :], sem).start()


def _moe_kernel(be_ref, nv_ref, tok_ref, tok_next_ref, x_hbm, wg_ref, wu_ref, wd_ref, o_ref,
                xg_ref, xb_ref, sem, *, blk):
    del be_ref
    i = pl.program_id(0)
    f = pl.program_id(1)
    n_valid = nv_ref[0]
    live = i < n_valid
    slot = i % 2

    @pl.when(f == 0)
    def _():
        o_ref[...] = jnp.zeros_like(o_ref)

    @pl.when((i == 0) & (f == 0))
    def _():
        _gather_rows(x_hbm, tok_ref, xg_ref.at[0], sem.at[0], blk, wait=False)

    @pl.when(live & (f == 0))
    def _():
        _gather_rows(x_hbm, tok_ref, xg_ref.at[slot], sem.at[slot], blk, wait=True)
        xb_ref[...] = xg_ref[slot].astype(BF16)

    @pl.when((i + 1 < n_valid) & (f == 0))
    def _():
        _gather_rows(x_hbm, tok_next_ref, xg_ref.at[1 - slot], sem.at[1 - slot], blk, wait=False)

    @pl.when(live)
    def _():
        xb = xb_ref[...]
        gt = jnp.dot(xb, wg_ref[...], preferred_element_type=F32)
        up = jnp.dot(xb, wu_ref[...], preferred_element_type=F32)
        h = (gt * _sigmoid(gt) * up).astype(BF16)
        o_ref[...] += jnp.dot(h, wd_ref[...], preferred_element_type=F32)


def _moe_ffn(block_expert, n_valid, slot_token, x, wg, wu, wd, blk, tf):
    T, D = x.shape
    F = wg.shape[2]
    n_blocks = block_expert.shape[0]
    nf = F // tf
    tok3 = slot_token.reshape(n_blocks, 1, blk)

    def f_eff(i, f, nv):
        return jnp.where(i < nv[0], f, nf - 1)

    grid_spec = pltpu.PrefetchScalarGridSpec(
        num_scalar_prefetch=2,
        grid=(n_blocks, nf),
        in_specs=[pl.BlockSpec((1, 1, blk), lambda i, f, be, nv: (i, 0, 0), memory_space=pltpu.SMEM),
                  pl.BlockSpec((1, 1, blk), lambda i, f, be, nv: (jnp.minimum(i + 1, n_blocks - 1), 0, 0),
                               memory_space=pltpu.SMEM),
                  pl.BlockSpec(memory_space=pl.ANY),
                  pl.BlockSpec((None, D, tf), lambda i, f, be, nv: (be[i], 0, f_eff(i, f, nv))),
                  pl.BlockSpec((None, D, tf), lambda i, f, be, nv: (be[i], 0, f_eff(i, f, nv))),
                  pl.BlockSpec((None, tf, D), lambda i, f, be, nv: (be[i], f_eff(i, f, nv), 0))],
        out_specs=pl.BlockSpec((blk, D), lambda i, f, be, nv: (i, 0)),
        scratch_shapes=[pltpu.VMEM((2, blk, D), F32), pltpu.VMEM((blk, D), BF16),
                        pltpu.SemaphoreType.DMA((2,))],
    )
    return pl.pallas_call(
        functools.partial(_moe_kernel, blk=blk),
        grid_spec=grid_spec,
        out_shape=jax.ShapeDtypeStruct((n_blocks * blk, D), F32),
        compiler_params=_cparams("arbitrary", "arbitrary"),
        name="moe_ffn",
    )(block_expert, n_valid, tok3, tok3, x, wg, wu, wd)


def _combine_kernel(dest_ref, dest_next_ref, ys_hbm, x_ref, gate_ref, g_ref, b_ref, o_ref,
                    buf_ref, sem, *, tm):
    i = pl.program_id(0)
    slot = i % 2
    n = TOP_K * tm

    @pl.when(i == 0)
    def _():
        _gather_rows(ys_hbm, dest_ref, buf_ref.at[0], sem.at[0], n, wait=False)

    @pl.when(i + 1 < pl.num_programs(0))
    def _():
        _gather_rows(ys_hbm, dest_next_ref, buf_ref.at[1 - slot], sem.at[1 - slot], n, wait=False)

    _gather_rows(ys_hbm, dest_ref, buf_ref.at[slot], sem.at[slot], n, wait=True)
    gate = gate_ref[...]
    ff = buf_ref[slot, 0:tm, :] * gate[:, 0:1] + buf_ref[slot, tm:2 * tm, :] * gate[:, 1:2]
    o_ref[...] = _layer_norm(ALPHA * x_ref[...] + ff, g_ref[...], b_ref[...])


def _combine(dest, ys, x, gate, g, b, tm):
    T, D = x.shape
    nt = T // tm
    dest3 = dest.reshape(nt, tm, TOP_K).transpose(0, 2, 1).reshape(nt, 1, TOP_K * tm)
    return pl.pallas_call(
        functools.partial(_combine_kernel, tm=tm),
        grid=(nt,),
        in_specs=[pl.BlockSpec((1, 1, TOP_K * tm), lambda i: (i, 0, 0), memory_space=pltpu.SMEM),
                  pl.BlockSpec((1, 1, TOP_K * tm), lambda i: (jnp.minimum(i + 1, nt - 1), 0, 0),
                               memory_space=pltpu.SMEM),
                  pl.BlockSpec(memory_space=pl.ANY),
                  pl.BlockSpec((tm, D), lambda i: (i, 0)),
                  pl.BlockSpec((tm, LANES), lambda i: (i, 0)),
                  pl.BlockSpec((1, D), lambda i: (0, 0)),
                  pl.BlockSpec((1, D), lambda i: (0, 0))],
        out_specs=pl.BlockSpec((tm, D), lambda i: (i, 0)),
        out_shape=jax.ShapeDtypeStruct((T, D), F32),
        scratch_shapes=[pltpu.VMEM((2, TOP_K * tm, D), F32), pltpu.SemaphoreType.DMA((2,))],
        compiler_params=_cparams("arbitrary"),
        name="moe_combine",
    )(dest3, dest3, ys, x, gate, g, b)


def _dispatch(idx2, blk, n_blocks):
    T = idx2.shape[0]
    n_assign = T * TOP_K
    expert = idx2.reshape(-1)
    onehot = (expert[:, None] == jnp.arange(N_EXPERTS, dtype=jnp.int32)[None, :]).astype(jnp.int32)
    csum = jnp.cumsum(onehot, axis=0)
    rank = jnp.sum(onehot * (csum - 1), axis=1)
    counts = csum[-1]
    padded = (counts + blk - 1) // blk * blk
    padded_end = jnp.cumsum(padded)
    dest = (padded_end - padded)[expert] + rank
    token = jnp.arange(n_assign, dtype=jnp.int32) // TOP_K
    slot_token = jnp.zeros((n_blocks * blk,), jnp.int32).at[dest].set(token)
    block_expert = jnp.minimum(
        jnp.searchsorted(padded_end, jnp.arange(n_blocks, dtype=jnp.int32) * blk, side="right"),
        N_EXPERTS - 1).astype(jnp.int32)
    n_valid = (padded_end[-1:] // blk).astype(jnp.int32)
    block_ids = jnp.arange(n_blocks, dtype=jnp.int32)
    block_expert = jnp.where(block_ids < n_valid[0], block_expert,
                             block_expert[jnp.maximum(n_valid[0] - 1, 0)])
    return dest.reshape(T, TOP_K).astype(jnp.int32), slot_token, block_expert, n_valid


def _tile(n, pref):
    t = min(n, pref)
    while n % t:
        t //= 2
    return t


def kernel(x, w_in, w_out, hg_lb_logits, hg_norm_g, ml_conv_w, ml_conv_b, ml_b_i, ml_b_f, ml_norm_g,
           ln_mix_g, ln_mix_b, ln_ffn_g, ln_ffn_b, ffn_w_gate, ffn_w_up, ffn_w_down,
           moe_w_router, moe_w_gate, moe_w_up, moe_w_down):
    B, S, D = x.shape
    T = B * S
    depth = w_in.shape[0]
    hg_w = HG_HEADS * HG_D
    main_cols = 4 * hg_w + 2 * ML_HEADS * ML_DQK + 2 * ML_HEADS * ML_DV

    p = jax.nn.softmax(hg_lb_logits.astype(F32), axis=0)
    lower_bounds = jnp.cumsum(p, axis=0) - p[0]

    rows = _tile(S, 256)
    tm_proj = _tile(T, 1024)
    tm = _tile(T, 512)
    blk = _tile(T, 512)
    n_blocks = -(-(T * TOP_K + N_EXPERTS * (blk - 1)) // blk)
    wt = 1024

    f_dense = ffn_w_gate.shape[2]
    f_moe = moe_w_gate.shape[3]
    tf_dense = 512
    f_dense_p = -(-f_dense // tf_dense) * tf_dense

    w_gates_all = jnp.pad(lax.slice_in_dim(w_in, main_cols, w_in.shape[2], axis=2),
                          ((0, 0), (0, 0), (0, LANES - 2 * ML_HEADS))).astype(BF16)

    n_layers_moe = moe_w_gate.shape[0]
    moe_f32 = {"gate": moe_w_gate.reshape(n_layers_moe, N_EXPERTS * D, f_moe),
               "up": moe_w_up.reshape(n_layers_moe, N_EXPERTS * D, f_moe),
               "down": moe_w_down.reshape(n_layers_moe, N_EXPERTS * f_moe, D)}
    moe_bf16 = {}

    def moe_weight(name, j):
        w = moe_bf16.get((name, j))
        if w is None:
            _, R, C = moe_f32[name].shape
            w = _cast_bf16(moe_f32[name], j, R, C, wt, wt)
        return w.reshape((N_EXPERTS, D, f_moe) if name != "down" else (N_EXPERTS, f_moe, D))

    xt = x.reshape(T, D).astype(F32)
    for layer in range(depth):
        moe = layer % 2 == 1
        j = layer // 2
        if moe:
            riders = [("down", j)]
        elif layer + 1 < depth:
            riders = [("gate", j), ("up", j)]
        else:
            riders = []

        w_main = w_in[layer, :, :main_cols].astype(BF16)
        proj, gates = _proj(xt, w_main, w_gates_all[layer], tm_proj, main_cols // 4)

        gate_bias = jnp.pad(jnp.concatenate([ml_b_i[layer], ml_b_f[layer]]).astype(F32),
                            (0, LANES - 2 * ML_HEADS))[None, :]
        hg, ml, casts = _mix(proj, gates, lower_bounds[layer][None, :], hg_norm_g[layer][None, :].astype(F32),
                             ml_conv_w[layer].astype(F32), ml_conv_b[layer][None, :].astype(F32), gate_bias,
                             ml_norm_g[layer][None, :].astype(F32), B, S, rows, 4 * hg_w,
                             cast_srcs=[(moe_f32[name], jj) for name, jj in riders])
        for key, out in zip(riders, casts):
            if out is not None:
                moe_bf16[key] = out

        w_router = None
        if moe:
            wr = jnp.pad(moe_w_router[j].astype(F32), ((0, 0), (0, LANES - N_EXPERTS)))
            wr_hi = wr.astype(BF16)
            w_router = jnp.concatenate([wr_hi, (wr - wr_hi.astype(F32)).astype(BF16)], axis=1)
        res = _outproj(hg, ml, _cast_bf16(w_out, layer, D, D, wt, wt), xt,
                       ln_mix_g[layer][None, :], ln_mix_b[layer][None, :], w_router, tm)
        g2 = ln_ffn_g[layer][None, :]
        b2 = ln_ffn_b[layer][None, :]
        if not moe:
            (x1,) = res
            wg = _cast_bf16(ffn_w_gate, j, D, f_dense_p, wt, tf_dense)
            wu = _cast_bf16(ffn_w_up, j, D, f_dense_p, wt, tf_dense)
            wd = _cast_bf16(ffn_w_down, j, f_dense_p, D, tf_dense, wt)
            xt = _ffn(x1, wg, wu, wd, g2, b2, tm, tf_dense)
        else:
            x1, idx, gate = res
            dest, slot_token, block_expert, n_valid = _dispatch(idx[:, :TOP_K], blk, n_blocks)
            ys = _moe_ffn(block_expert, n_valid, slot_token, x1, moe_weight("gate", j), moe_weight("up", j),
                          moe_weight("down", j), blk, 1024)
            xt = _combine(dest, ys, x1, gate, g2, b2, _tile(T, 256))
    return xt.reshape(B, S, D).astype(x.dtype)
```

```python
import functools

import jax
import jax.numpy as jnp
import numpy as np
from jax import lax
from jax.experimental import pallas as pl
from jax.experimental.pallas import tpu as pltpu

F32 = jnp.float32
BF16 = jnp.bfloat16
HIGHEST = lax.Precision.HIGHEST

MODEL_DEPTH = 4
HG_HEADS = 8
HG_D = 128
ML_HEADS = 4
ML_DQK = 128
ML_DV = 256
CONV_W = 4
N_EXPERTS = 8
TOP_K = 2
ALPHA = (2 * MODEL_DEPTH) ** 0.25
LN_EPS = 1e-5
RMS_EPS = 1e-6
NEG_BIG = -1e30
F_MIN = 1e-30

LANES = 128
SUBLANES = 8
VMEM_LIMIT_BYTES = 56 * 1024 * 1024
SIDE_CAST_BLOCK_BYTES = 4 * 1024 * 1024

CHUNK = 64
CONV_PAD = SUBLANES


def _cparams(*sem):
    return pltpu.CompilerParams(dimension_semantics=sem, vmem_limit_bytes=VMEM_LIMIT_BYTES)


def _sigmoid(x):
    return 1.0 / (1.0 + jnp.exp(-x))


def _layer_norm(y, g, b):
    mu = jnp.mean(y, axis=-1, keepdims=True)
    yc = y - mu
    var = jnp.mean(yc * yc, axis=-1, keepdims=True)
    return yc * lax.rsqrt(var + LN_EPS) * g + b


def _cast_kernel(x_ref, o_ref, *, rows_in, cols_in, mask):
    x = x_ref[...]
    if mask:
        tr, tc = x.shape
        r = pl.program_id(0) * tr + lax.broadcasted_iota(jnp.int32, x.shape, 0)
        c = pl.program_id(1) * tc + lax.broadcasted_iota(jnp.int32, x.shape, 1)
        x = jnp.where((r < rows_in) & (c < cols_in), x, 0.0)
    o_ref[...] = x.astype(BF16)


def _cast_bf16(w, layer, rows_out, cols_out, tr, tc):
    _, R, C = w.shape
    mask = rows_out > R or cols_out > C
    return pl.pallas_call(
        functools.partial(_cast_kernel, rows_in=R, cols_in=C, mask=mask),
        grid=(rows_out // tr, cols_out // tc),
        in_specs=[pl.BlockSpec((None, tr, tc), lambda i, j: (layer, i, j))],
        out_specs=pl.BlockSpec((tr, tc), lambda i, j: (i, j)),
        out_shape=jax.ShapeDtypeStruct((rows_out, cols_out), BF16),
        compiler_params=_cparams("parallel", "parallel"),
        name="cast_bf16",
    )(w)


def _proj_kernel(x_ref, w_ref, wgate_ref, o_ref, ogate_ref):
    xb = x_ref[...].astype(BF16)
    o_ref[...] = jnp.dot(xb, w_ref[...], preferred_element_type=F32)

    @pl.when(pl.program_id(1) == 0)
    def _():
        ogate_ref[...] = jnp.dot(xb, wgate_ref[...], preferred_element_type=F32)


def _proj(x, w, w_gates, tm, tn):
    T, D = x.shape
    N = w.shape[1]
    return pl.pallas_call(
        _proj_kernel,
        grid=(T // tm, N // tn),
        in_specs=[pl.BlockSpec((tm, D), lambda i, j: (i, 0)),
                  pl.BlockSpec((D, tn), lambda i, j: (0, j)),
                  pl.BlockSpec((D, LANES), lambda i, j: (0, 0))],
        out_specs=[pl.BlockSpec((tm, tn), lambda i, j: (i, j)),
                   pl.BlockSpec((tm, LANES), lambda i, j: (i, 0))],
        out_shape=[jax.ShapeDtypeStruct((T, N), F32), jax.ShapeDtypeStruct((T, LANES), F32)],
        compiler_params=_cparams("parallel", "arbitrary"),
        name="in_proj",
    )(x, w, w_gates)


_HG_LEVELS = (32, 16, 8, 4, 2, 1)
_HG_DIAG = len(_HG_LEVELS)


def _hgrn2_tables():
    t = np.arange(CHUNK)[:, None]
    s = np.arange(CHUNK)[None, :]
    level = np.full((CHUNK, CHUNK), -1, np.int32)
    level[t == s] = _HG_DIAG
    tril = (s <= t).astype(np.float32)
    mats = [tril]
    for li, m in enumerate(_HG_LEVELS):
        mask = (t // (2 * m) == s // (2 * m)) & (t % (2 * m) >= m) & (s % (2 * m) < m)
        assert not (level[mask] >= 0).any()
        level[mask] = li
        if m < SUBLANES:
            mid = (t // (2 * m)) * (2 * m) + m
            mats.append(tril - (s <= mid - 1).astype(np.float32))
    assert (level[s < t] >= 0).all() and (level[s > t] == -1).all()
    return np.concatenate(mats, axis=0), level


def _mix_kernel(hq_ref, hf_ref, hi_ref, hg_ref, lb_ref, hng_ref, w_ref, lv_ref,
                mq_ref, mk_ref, mv_ref, mog_ref, gt_ref, cw_ref, cb_ref, gb_ref, mng_ref, *rest,
                n_chunks, n_side):
    side_in = rest[:n_side]
    ho_ref, mo_ref = rest[n_side:n_side + 2]
    side_out = rest[n_side + 2:2 * n_side + 2]
    st_ref, cq_ref, ck_ref, c_ref, n_ref, m_ref = rest[2 * n_side + 2:]
    for src_ref, dst_ref in zip(side_in, side_out):
        dst_ref[...] = src_ref[...].astype(BF16)

    rows = n_chunks * CHUNK
    W = ML_HEADS * ML_DQK
    nt = (((1,), (1,)), ((), ()))

    @pl.when(pl.program_id(1) == 0)
    def _():
        st_ref[...] = jnp.zeros_like(st_ref)
        cq_ref[0:CONV_PAD, :] = jnp.zeros((CONV_PAD, W), F32)
        ck_ref[0:CONV_PAD, :] = jnp.zeros((CONV_PAD, W), F32)
        c_ref[...] = jnp.zeros_like(c_ref)
        n_ref[...] = jnp.zeros_like(n_ref)
        m_ref[...] = jnp.zeros_like(m_ref)

    cq_ref[CONV_PAD:CONV_PAD + rows, :] = mq_ref[...]
    ck_ref[CONV_PAD:CONV_PAD + rows, :] = mk_ref[...]

    w = w_ref[...]
    lv = lv_ref[...]
    ones_bf = jnp.ones((HG_D, CHUNK), BF16)
    units = [(hh, c) for hh in range(HG_HEADS) for c in range(n_chunks)]
    rows_of = {u: slice(u[1] * CHUNK, (u[1] + 1) * CHUNK) for u in units}
    lanes_of = {u: slice(u[0] * HG_D, (u[0] + 1) * HG_D) for u in units}
    q, k, v_bf, wb, p, qd, upd, dec, o = {}, {}, {}, {}, {}, {}, {}, {}, {}

    def h_phase1(u):
        lb = lb_ref[:, lanes_of[u]]
        one_m_lb = 1.0 - lb
        qp = hq_ref[rows_of[u], lanes_of[u]]
        fp = hf_ref[rows_of[u], lanes_of[u]]
        q[u] = qp * _sigmoid(qp)
        e_f = jnp.exp(-fp)
        sig_f = 1.0 / (1.0 + e_f)
        log_f = jnp.log2(jnp.maximum(lb + one_m_lb * sig_f, F_MIN))
        k[u] = one_m_lb * (e_f * sig_f)
        v_bf[u] = hi_ref[rows_of[u], lanes_of[u]].astype(BF16)
        lf_hi = log_f.astype(BF16)
        lf_lo = (log_f - lf_hi.astype(F32)).astype(BF16)
        wb[u] = (jnp.dot(w, lf_hi, preferred_element_type=F32)
                 + jnp.dot(w, lf_lo, preferred_element_type=F32))

    def h_phase2(u):
        b = wb[u][0:CHUNK, :]
        b_last = b[CHUNK - 1:CHUNK, :]
        qd[u] = (q[u] * jnp.exp2(b)).astype(BF16)
        kd = k[u] * jnp.exp2(b_last - b)
        upd[u] = jnp.dot(hi_ref[rows_of[u], lanes_of[u]].T.astype(BF16), kd.astype(BF16),
                         preferred_element_type=F32)
        dec[u] = jnp.exp2(b_last)
        pc = jnp.where(lv == _HG_DIAG,
                       jnp.dot((q[u] * k[u]).astype(BF16), ones_bf, preferred_element_type=F32), 0.0)
        small = 0
        for li, m in enumerate(_HG_LEVELS):
            if m >= SUBLANES:
                zero = jnp.zeros((m, HG_D), F32)
                qs, ks = [], []
                for lo in range(0, CHUNK, 2 * m):
                    mid = lo + m
                    ref = b[mid - 1:mid, :]
                    ks += [k[u][lo:mid, :] * jnp.exp2(ref - b[lo:mid, :]), zero]
                    qs += [zero, q[u][mid:mid + m, :] * jnp.exp2(b[mid:mid + m, :] - ref)]
                qt = jnp.concatenate(qs, axis=0)
                kt = jnp.concatenate(ks, axis=0)
            else:
                small += 1
                d = wb[u][small * CHUNK:(small + 1) * CHUNK, :]
                qt = q[u] * jnp.exp2(jnp.minimum(d, 0.0))
                kt = k[u] * jnp.exp2(jnp.minimum(-d, 0.0))
            s_l = lax.dot_general(qt.astype(BF16), kt.astype(BF16), nt, preferred_element_type=F32)
            pc = jnp.where(lv == li, s_l, pc)
        p[u] = pc.astype(BF16)

    def h_phase3(hh):
        st = st_ref[hh]
        for c in range(n_chunks):
            u = (hh, c)
            o[u] = lax.dot_general(qd[u], st.astype(BF16), nt, preferred_element_type=F32)
            st = st * dec[u] + upd[u]
        st_ref[hh] = st

    def h_phase4(u):
        oc = o[u] + jnp.dot(p[u], v_bf[u], preferred_element_type=F32)
        gp = hg_ref[rows_of[u], lanes_of[u]]
        on = oc * lax.rsqrt(jnp.mean(oc * oc, axis=-1, keepdims=True) + RMS_EPS) * hng_ref[:, lanes_of[u]]
        ho_ref[rows_of[u], lanes_of[u]] = (on * (gp * _sigmoid(gp))).astype(ho_ref.dtype)

    row = lax.broadcasted_iota(jnp.int32, (CHUNK, CHUNK), 0)
    col = lax.broadcasted_iota(jnp.int32, (CHUNK, CHUNK), 1)
    causal = col <= row
    tril = causal.astype(F32)
    lane = lax.broadcasted_iota(jnp.int32, (CHUNK, LANES), 1)
    k_scale = ML_DQK ** -0.5
    xs, xts, mst = {}, {}, {}

    def m_gates(c):
        pre = gt_ref[c * CHUNK:(c + 1) * CHUNK, :] + gb_ref[...]
        log_f = jnp.minimum(pre, 0.0) - jnp.log1p(jnp.exp(-jnp.abs(pre)))
        bcum = jnp.dot(tril, log_f, precision=HIGHEST, preferred_element_type=F32)
        xs[c] = jnp.where(lane < ML_HEADS, pre, bcum)
        xts[c] = xs[c].T

    def m_first(c, h):
        r0 = c * CHUNK
        cs = slice(h * ML_DQK, (h + 1) * ML_DQK)
        vs = slice(h * ML_DV, (h + 1) * ML_DV)

        def conv(src_ref, w_off):
            acc = cb_ref[0:1, w_off + h * ML_DQK:w_off + (h + 1) * ML_DQK]
            for j in range(CONV_W):
                start = CONV_PAD + r0 - (CONV_W - 1) + j
                acc = acc + (cw_ref[j:j + 1, w_off + h * ML_DQK:w_off + (h + 1) * ML_DQK]
                             * src_ref[start:start + CHUNK, cs])
            return acc * _sigmoid(acc)

        qq = conv(cq_ref, 0)
        kk = conv(ck_ref, W) * k_scale
        v_b = mv_ref[r0:r0 + CHUNK, vs].astype(BF16)
        q_b = qq.astype(BF16)
        x, xt = xs[c], xts[c]
        li_c = x[:, h:h + 1]
        b_c = x[:, ML_HEADS + h:ML_HEADS + h + 1]
        li_r = xt[h:h + 1, :]
        b_r = xt[ML_HEADS + h:ML_HEADS + h + 1, :]
        log_intra = jnp.where(causal, b_c - b_r + li_r, NEG_BIG)
        mst[c, h] = dict(
            q=qq, k=kk, v_b=v_b, li_c=li_c, b_c=b_c, log_intra=log_intra,
            rmax=jnp.max(log_intra, axis=-1, keepdims=True),
            qk=lax.dot_general(q_b, kk.astype(BF16), nt, preferred_element_type=F32),
            qc=jnp.dot(q_b, c_ref[h].astype(BF16), preferred_element_type=F32),
            qn=jnp.sum(qq * n_ref[h][0:1, :], axis=-1, keepdims=True))

    def m_second(c, h):
        r0 = c * CHUNK
        vs = slice(h * ML_DV, (h + 1) * ML_DV)
        d = mst.pop((c, h))
        b_c, li_c = d["b_c"], d["li_c"]
        b_last = b_c[CHUNK - 1:CHUNK, :]
        m_prev = m_ref[h][0:1, 0:1]
        log_inter = b_c + m_prev
        m_t = jnp.maximum(log_inter, d["rmax"])
        w_inter = jnp.exp(log_inter - m_t)
        scores = d["qk"] * jnp.exp(d["log_intra"] - m_t)
        num = jnp.dot(scores.astype(BF16), d["v_b"], preferred_element_type=F32) + w_inter * d["qc"]
        den = jnp.sum(scores, axis=-1, keepdims=True) + w_inter * d["qn"]
        hh = num / jnp.maximum(jnp.abs(den), jnp.exp(-m_t))
        m_new = m_t[CHUNK - 1:CHUNK, :]
        kw = d["k"] * jnp.exp(b_last - b_c + li_c - m_new)
        w_carry = jnp.exp(b_last + m_prev - m_new)
        c_ref[h] = w_carry * c_ref[h] + jnp.dot(kw.T.astype(BF16), d["v_b"], preferred_element_type=F32)
        n_new = w_carry * n_ref[h][0:1, :] + jnp.sum(kw, axis=0, keepdims=True)
        n_ref[h] = jnp.broadcast_to(n_new, (SUBLANES, ML_DQK))
        m_ref[h] = jnp.broadcast_to(m_new, (SUBLANES, LANES))
        hn = hh * lax.rsqrt(jnp.mean(hh * hh, axis=-1, keepdims=True) + RMS_EPS) * mng_ref[0:1, vs]
        og = mog_ref[r0:r0 + CHUNK, vs]
        mo_ref[r0:r0 + CHUNK, vs] = (_sigmoid(og) * hn).astype(mo_ref.dtype)

    pairs = [(c, h) for c in range(n_chunks) for h in range(ML_HEADS)]
    for c in range(n_chunks):
        m_gates(c)
    h_pieces = ([functools.partial(h_phase1, u) for u in units]
                + [functools.partial(h_phase2, u) for u in units])
    m_pieces = []
    for i in range(len(pairs) + 1):
        if i < len(pairs):
            m_pieces.append(functools.partial(m_first, *pairs[i]))
        if i >= 1:
            m_pieces.append(functools.partial(m_second, *pairs[i - 1]))
    early, late = m_pieces[:len(m_pieces) // 2], m_pieces[len(m_pieces) // 2:]
    per = len(h_pieces) // len(early)
    for i, m_piece in enumerate(early):
        m_piece()
        for h_piece in h_pieces[i * per:(i + 1) * per]:
            h_piece()
    for h_piece in h_pieces[len(early) * per:]:
        h_piece()
    for hh in range(HG_HEADS):
        h_phase3(hh)
    per = len(units) // len(late)
    for i, m_piece in enumerate(late):
        m_piece()
        for u in units[i * per:(i + 1) * per]:
            h_phase4(u)
    for u in units[len(late) * per:]:
        h_phase4(u)
    cq_ref[0:CONV_PAD, :] = cq_ref[rows:rows + CONV_PAD, :]
    ck_ref[0:CONV_PAD, :] = ck_ref[rows:rows + CONV_PAD, :]


def _side_cast_rows(cast_src, n_steps):
    _, R, C = cast_src[0].shape
    rc = R // n_steps
    ok = R % n_steps == 0 and rc % 16 == 0 and C % LANES == 0 and rc * C * 4 <= SIDE_CAST_BLOCK_BYTES
    return rc if ok else 0


def _mix(proj, gates, lb, hg_ng, conv_w, conv_b, gate_bias, ml_ng, B, S, rows, col0, cast_srcs=()):
    T = B * S
    ns = S // rows
    n_chunks = rows // CHUNK
    wd = HG_HEADS * HG_D
    w_np, level_np = _hgrn2_tables()
    w = jnp.asarray(w_np, BF16)
    level = jnp.asarray(level_np, jnp.int32)
    W = ML_HEADS * ML_DQK
    V = ML_HEADS * ML_DV
    qb = col0 // W
    vb = (col0 + 2 * W) // V
    riding = [(i, src, _side_cast_rows(src, B * ns)) for i, src in enumerate(cast_srcs)]
    riding = [(i, src, rc) for i, src, rc in riding if rc]

    def rowmap(cb):
        return lambda b, j, cb=cb: (b * ns + j, cb)

    const = lambda b, j: (0, 0)
    in_specs = [pl.BlockSpec((rows, wd), rowmap(0)), pl.BlockSpec((rows, wd), rowmap(1)),
                pl.BlockSpec((rows, wd), rowmap(2)), pl.BlockSpec((rows, wd), rowmap(3)),
                pl.BlockSpec((1, wd), const), pl.BlockSpec((1, wd), const),
                pl.BlockSpec(w.shape, const), pl.BlockSpec(level.shape, const),
                pl.BlockSpec((rows, W), rowmap(qb)), pl.BlockSpec((rows, W), rowmap(qb + 1)),
                pl.BlockSpec((rows, V), rowmap(vb)), pl.BlockSpec((rows, V), rowmap(vb + 1)),
                pl.BlockSpec((rows, LANES), rowmap(0)),
                pl.BlockSpec((CONV_W, 2 * W), const), pl.BlockSpec((1, 2 * W), const),
                pl.BlockSpec((1, LANES), const), pl.BlockSpec((1, V), const)]
    out_specs = [pl.BlockSpec((rows, wd), rowmap(0)), pl.BlockSpec((rows, V), rowmap(0))]
    out_shape = [jax.ShapeDtypeStruct((T, wd), BF16), jax.ShapeDtypeStruct((T, V), BF16)]
    args = [proj, proj, proj, proj, lb, hg_ng, w, level, proj, proj, proj, proj, gates,
            conv_w, conv_b, gate_bias, ml_ng]
    for _, (w3d, w_layer), rc in riding:
        in_specs.append(pl.BlockSpec((None, rc, w3d.shape[2]), lambda b, j, wl=w_layer: (wl, b * ns + j, 0)))
        out_specs.append(pl.BlockSpec((rc, w3d.shape[2]), rowmap(0)))
        out_shape.append(jax.ShapeDtypeStruct(w3d.shape[1:], BF16))
        args.append(w3d)
    res = pl.pallas_call(
        functools.partial(_mix_kernel, n_chunks=n_chunks, n_side=len(riding)),
        grid=(B, ns),
        in_specs=in_specs, out_specs=out_specs, out_shape=out_shape,
        scratch_shapes=[pltpu.VMEM((HG_HEADS, HG_D, HG_D), F32),
                        pltpu.VMEM((rows + 2 * CONV_PAD, W), F32),
                        pltpu.VMEM((rows + 2 * CONV_PAD, W), F32),
                        pltpu.VMEM((ML_HEADS, ML_DQK, ML_DV), F32),
                        pltpu.VMEM((ML_HEADS, SUBLANES, ML_DQK), F32),
                        pltpu.VMEM((ML_HEADS, SUBLANES, LANES), F32)],
        compiler_params=_cparams("parallel", "arbitrary"),
        name="mixers",
    )(*args)
    casts = [None] * len(cast_srcs)
    for (i, _, _), out in zip(riding, res[2:]):
        casts[i] = out
    return res[0], res[1], casts


def _top2(logits):
    lane = lax.broadcasted_iota(jnp.int32, logits.shape, 1)
    lg = jnp.where(lane < N_EXPERTS, logits, -jnp.inf)
    m0 = jnp.max(lg, axis=-1, keepdims=True)
    i0 = jnp.min(jnp.where(lg == m0, lane, LANES), axis=-1, keepdims=True)
    lg1 = jnp.where(lane == i0, -jnp.inf, lg)
    m1 = jnp.max(lg1, axis=-1, keepdims=True)
    i1 = jnp.min(jnp.where(lg1 == m1, lane, LANES), axis=-1, keepdims=True)
    e1 = jnp.exp(m1 - m0)
    g0 = 1.0 / (1.0 + e1)
    g1 = e1 / (1.0 + e1)
    idx = jnp.where(lane == 0, i0, jnp.where(lane == 1, i1, 0))
    gate = jnp.where(lane == 0, g0, jnp.where(lane == 1, g1, 0.0))
    return idx, gate


def _outproj_kernel(hg_ref, ml_ref, w_ref, x_ref, g_ref, b_ref, *rest, route):
    if route:
        wr_ref, o_ref, idx_ref, gate_ref = rest
    else:
        (o_ref,) = rest
    half = hg_ref.shape[1]
    tm = x_ref.shape[0]
    parts = [slice(0, tm // 2), slice(tm // 2, tm)]
    mixed = [jnp.dot(hg_ref[r, :], w_ref[0:half, :], preferred_element_type=F32)
             + jnp.dot(ml_ref[r, :], w_ref[half:, :], preferred_element_type=F32) for r in parts]
    y = jnp.concatenate([_layer_norm(ALPHA * x_ref[r, :] + m, g_ref[...], b_ref[...])
                         for r, m in zip(parts, mixed)], axis=0)
    o_ref[...] = y
    if route:
        y_hi = y.astype(BF16)
        y_lo = (y - y_hi.astype(F32)).astype(BF16)
        r_hi = jnp.dot(y_hi, wr_ref[...], preferred_element_type=F32)
        r_lo = jnp.dot(y_lo, wr_ref[:, 0:LANES], preferred_element_type=F32)
        logits = r_hi[:, 0:LANES] + r_hi[:, LANES:] + r_lo
        idx, gate = _top2(logits)
        idx_ref[...] = idx
        gate_ref[...] = gate


def _outproj(hg, ml, w, x, g, b, w_router, tm):
    T, D = x.shape
    half = hg.shape[1]
    route = w_router is not None
    row = lambda i: (i, 0)
    const = lambda i: (0, 0)
    in_specs = [pl.BlockSpec((tm, half), row), pl.BlockSpec((tm, half), row),
                pl.BlockSpec((D, D), const), pl.BlockSpec((tm, D), row),
                pl.BlockSpec((1, D), const), pl.BlockSpec((1, D), const)]
    out_specs = [pl.BlockSpec((tm, D), row)]
    out_shape = [jax.ShapeDtypeStruct((T, D), F32)]
    args = [hg, ml, w, x, g, b]
    if route:
        in_specs.append(pl.BlockSpec((D, 2 * LANES), const))
        out_specs += [pl.BlockSpec((tm, LANES), row), pl.BlockSpec((tm, LANES), row)]
        out_shape += [jax.ShapeDtypeStruct((T, LANES), jnp.int32),
                      jax.ShapeDtypeStruct((T, LANES), F32)]
        args.append(w_router)
    return pl.pallas_call(
        functools.partial(_outproj_kernel, route=route),
        grid=(T // tm,),
        in_specs=in_specs, out_specs=out_specs, out_shape=out_shape,
        compiler_params=_cparams("parallel"),
        name="out_proj_route" if route else "out_proj",
    )(*args)


def _ffn_kernel(x_ref, wg_ref, wu_ref, wd_ref, g_ref, b_ref, o_ref, xb_ref):
    f = pl.program_id(1)

    @pl.when(f == 0)
    def _():
        xb_ref[...] = x_ref[...].astype(BF16)
        o_ref[...] = jnp.zeros_like(o_ref)

    xb = xb_ref[...]
    gt = jnp.dot(xb, wg_ref[...], preferred_element_type=F32)
    up = jnp.dot(xb, wu_ref[...], preferred_element_type=F32)
    h = (gt * _sigmoid(gt) * up).astype(BF16)
    o_ref[...] += jnp.dot(h, wd_ref[...], preferred_element_type=F32)

    @pl.when(f == pl.num_programs(1) - 1)
    def _():
        o_ref[...] = _layer_norm(ALPHA * x_ref[...] + o_ref[...], g_ref[...], b_ref[...])


def _ffn(x, wg, wu, wd, g, b, tm, tf):
    T, D = x.shape
    Fp = wg.shape[1]
    return pl.pallas_call(
        _ffn_kernel,
        grid=(T // tm, Fp // tf),
        in_specs=[pl.BlockSpec((tm, D), lambda i, f: (i, 0)),
                  pl.BlockSpec((D, tf), lambda i, f: (0, f)),
                  pl.BlockSpec((D, tf), lambda i, f: (0, f)),
                  pl.BlockSpec((tf, D), lambda i, f: (f, 0)),
                  pl.BlockSpec((1, D), lambda i, f: (0, 0)),
                  pl.BlockSpec((1, D), lambda i, f: (0, 0))],
        out_specs=pl.BlockSpec((tm, D), lambda i, f: (i, 0)),
        out_shape=jax.ShapeDtypeStruct((T, D), F32),
        scratch_shapes=[pltpu.VMEM((tm, D), BF16)],
        compiler_params=_cparams("parallel", "arbitrary"),
        name="dense_ffn",
    )(x, wg, wu, wd, g, b)


def _gather_rows(src_hbm, idx_ref, dst_ref, sem, n, *, wait):
    if wait:
        pltpu.make_async_copy(src_hbm.at[pl.ds(0, n), :], dst_ref, sem).wait()
        return
    for r in range(n):
        pltpu.make_async_copy(src_hbm.at[pl.ds(idx_ref[0, 0, r], 1), :],
                              dst_ref.at[pl.ds(r, 1), :], sem).start()


def _moe_kernel(be_ref, nv_ref, tok_ref, tok_next_ref, x_hbm, wg_ref, wu_ref, wd_ref, o_ref,
                xg_ref, xb_ref, sem, *, blk):
    del be_ref
    i = pl.program_id(0)
    f = pl.program_id(1)
    n_valid = nv_ref[0]
    live = i < n_valid
    slot = i % 2

    @pl.when(f == 0)
    def _():
        o_ref[...] = jnp.zeros_like(o_ref)

    @pl.when((i == 0) & (f == 0))
    def _():
        _gather_rows(x_hbm, tok_ref, xg_ref.at[0], sem.at[0], blk, wait=False)

    @pl.when(live & (f == 0))
    def _():
        _gather_rows(x_hbm, tok_ref, xg_ref.at[slot], sem.at[slot], blk, wait=True)
        xb_ref[...] = xg_ref[slot].astype(BF16)

    @pl.when((i + 1 < n_valid) & (f == 0))
    def _():
        _gather_rows(x_hbm, tok_next_ref, xg_ref.at[1 - slot], sem.at[1 - slot], blk, wait=False)

    @pl.when(live)
    def _():
        xb = xb_ref[...]
        gt = jnp.dot(xb, wg_ref[...], preferred_element_type=F32)
        up = jnp.dot(xb, wu_ref[...], preferred_element_type=F32)
        h = (gt * _sigmoid(gt) * up).astype(BF16)
        o_ref[...] += jnp.dot(h, wd_ref[...], preferred_element_type=F32)


def _moe_ffn(block_expert, n_valid, slot_token, x, wg, wu, wd, blk, tf):
    T, D = x.shape
    F = wg.shape[2]
    n_blocks = block_expert.shape[0]
    nf = F // tf
    tok3 = slot_token.reshape(n_blocks, 1, blk)

    def f_eff(i, f, nv):
        return jnp.where(i < nv[0], f, nf - 1)

    grid_spec = pltpu.PrefetchScalarGridSpec(
        num_scalar_prefetch=2,
        grid=(n_blocks, nf),
        in_specs=[pl.BlockSpec((1, 1, blk), lambda i, f, be, nv: (i, 0, 0), memory_space=pltpu.SMEM),
                  pl.BlockSpec((1, 1, blk), lambda i, f, be, nv: (jnp.minimum(i + 1, n_blocks - 1), 0, 0),
                               memory_space=pltpu.SMEM),
                  pl.BlockSpec(memory_space=pl.ANY),
                  pl.BlockSpec((None, D, tf), lambda i, f, be, nv: (be[i], 0, f_eff(i, f, nv))),
                  pl.BlockSpec((None, D, tf), lambda i, f, be, nv: (be[i], 0, f_eff(i, f, nv))),
                  pl.BlockSpec((None, tf, D), lambda i, f, be, nv: (be[i], f_eff(i, f, nv), 0))],
        out_specs=pl.BlockSpec((blk, D), lambda i, f, be, nv: (i, 0)),
        scratch_shapes=[pltpu.VMEM((2, blk, D), F32), pltpu.VMEM((blk, D), BF16),
                        pltpu.SemaphoreType.DMA((2,))],
    )
    return pl.pallas_call(
        functools.partial(_moe_kernel, blk=blk),
        grid_spec=grid_spec,
        out_shape=jax.ShapeDtypeStruct((n_blocks * blk, D), F32),
        compiler_params=_cparams("arbitrary", "arbitrary"),
        name="moe_ffn",
    )(block_expert, n_valid, tok3, tok3, x, wg, wu, wd)


def _combine_kernel(dest_ref, dest_next_ref, ys_hbm, x_ref, gate_ref, g_ref, b_ref, o_ref,
                    buf_ref, sem, *, tm):
    i = pl.program_id(0)
    slot = i % 2
    n = TOP_K * tm

    @pl.when(i == 0)
    def _():
        _gather_rows(ys_hbm, dest_ref, buf_ref.at[0], sem.at[0], n, wait=False)

    @pl.when(i + 1 < pl.num_programs(0))
    def _():
        _gather_rows(ys_hbm, dest_next_ref, buf_ref.at[1 - slot], sem.at[1 - slot], n, wait=False)

    _gather_rows(ys_hbm, dest_ref, buf_ref.at[slot], sem.at[slot], n, wait=True)
    gate = gate_ref[...]
    ff = buf_ref[slot, 0:tm, :] * gate[:, 0:1] + buf_ref[slot, tm:2 * tm, :] * gate[:, 1:2]
    o_ref[...] = _layer_norm(ALPHA * x_ref[...] + ff, g_ref[...], b_ref[...])


def _combine(dest, ys, x, gate, g, b, tm):
    T, D = x.shape
    nt = T // tm
    dest3 = dest.reshape(nt, tm, TOP_K).transpose(0, 2, 1).reshape(nt, 1, TOP_K * tm)
    return pl.pallas_call(
        functools.partial(_combine_kernel, tm=tm),
        grid=(nt,),
        in_specs=[pl.BlockSpec((1, 1, TOP_K * tm), lambda i: (i, 0, 0), memory_space=pltpu.SMEM),
                  pl.BlockSpec((1, 1, TOP_K * tm), lambda i: (jnp.minimum(i + 1, nt - 1), 0, 0),
                               memory_space=pltpu.SMEM),
                  pl.BlockSpec(memory_space=pl.ANY),
                  pl.BlockSpec((tm, D), lambda i: (i, 0)),
                  pl.BlockSpec((tm, LANES), lambda i: (i, 0)),
                  pl.BlockSpec((1, D), lambda i: (0, 0)),
                  pl.BlockSpec((1, D), lambda i: (0, 0))],
        out_specs=pl.BlockSpec((tm, D), lambda i: (i, 0)),
        out_shape=jax.ShapeDtypeStruct((T, D), F32),
        scratch_shapes=[pltpu.VMEM((2, TOP_K * tm, D), F32), pltpu.SemaphoreType.DMA((2,))],
        compiler_params=_cparams("arbitrary"),
        name="moe_combine",
    )(dest3, dest3, ys, x, gate, g, b)


def _dispatch(idx2, blk, n_blocks):
    T = idx2.shape[0]
    n_assign = T * TOP_K
    expert = idx2.reshape(-1)
    onehot = (expert[:, None] == jnp.arange(N_EXPERTS, dtype=jnp.int32)[None, :]).astype(jnp.int32)
    csum = jnp.cumsum(onehot, axis=0)
    rank = jnp.sum(onehot * (csum - 1), axis=1)
    counts = csum[-1]
    padded = (counts + blk - 1) // blk * blk
    padded_end = jnp.cumsum(padded)
    dest = (padded_end - padded)[expert] + rank
    token = jnp.arange(n_assign, dtype=jnp.int32) // TOP_K
    slot_token = jnp.zeros((n_blocks * blk,), jnp.int32).at[dest].set(token)
    block_expert = jnp.minimum(
        jnp.searchsorted(padded_end, jnp.arange(n_blocks, dtype=jnp.int32) * blk, side="right"),
        N_EXPERTS - 1).astype(jnp.int32)
    n_valid = (padded_end[-1:] // blk).astype(jnp.int32)
    block_ids = jnp.arange(n_blocks, dtype=jnp.int32)
    block_expert = jnp.where(block_ids < n_valid[0], block_expert,
                             block_expert[jnp.maximum(n_valid[0] - 1, 0)])
    return dest.reshape(T, TOP_K).astype(jnp.int32), slot_token, block_expert, n_valid


def _tile(n, pref):
    t = min(n, pref)
    while n % t:
        t //= 2
    return t


def kernel(x, w_in, w_out, hg_lb_logits, hg_norm_g, ml_conv_w, ml_conv_b, ml_b_i, ml_b_f, ml_norm_g,
           ln_mix_g, ln_mix_b, ln_ffn_g, ln_ffn_b, ffn_w_gate, ffn_w_up, ffn_w_down,
           moe_w_router, moe_w_gate, moe_w_up, moe_w_down):
    B, S, D = x.shape
    T = B * S
    depth = w_in.shape[0]
    hg_w = HG_HEADS * HG_D
    main_cols = 4 * hg_w + 2 * ML_HEADS * ML_DQK + 2 * ML_HEADS * ML_DV

    p = jax.nn.softmax(hg_lb_logits.astype(F32), axis=0)
    lower_bounds = jnp.cumsum(p, axis=0) - p[0]

    rows = _tile(S, 256)
    tm_proj = _tile(T, 1024)
    tm = _tile(T, 512)
    blk = _tile(T, 512)
    n_blocks = -(-(T * TOP_K + N_EXPERTS * (blk - 1)) // blk)
    wt = 1024

    f_dense = ffn_w_gate.shape[2]
    f_moe = moe_w_gate.shape[3]
    tf_dense = 512
    f_dense_p = -(-f_dense // tf_dense) * tf_dense

    w_gates_all = jnp.pad(lax.slice_in_dim(w_in, main_cols, w_in.shape[2], axis=2),
                          ((0, 0), (0, 0), (0, LANES - 2 * ML_HEADS))).astype(BF16)

    n_layers_moe = moe_w_gate.shape[0]
    moe_f32 = {"gate": moe_w_gate.reshape(n_layers_moe, N_EXPERTS * D, f_moe),
               "up": moe_w_up.reshape(n_layers_moe, N_EXPERTS * D, f_moe),
               "down": moe_w_down.reshape(n_layers_moe, N_EXPERTS * f_moe, D)}
    moe_bf16 = {}

    def moe_weight(name, j):
        w = moe_bf16.get((name, j))
        if w is None:
            _, R, C = moe_f32[name].shape
            w = _cast_bf16(moe_f32[name], j, R, C, wt, wt)
        return w.reshape((N_EXPERTS, D, f_moe) if name != "down" else (N_EXPERTS, f_moe, D))

    xt = x.reshape(T, D).astype(F32)
    for layer in range(depth):
        moe = layer % 2 == 1
        j = layer // 2
        if moe:
            riders = [("down", j)]
        elif layer + 1 < depth:
            riders = [("gate", j), ("up", j)]
        else:
            riders = []

        w_main = w_in[layer, :, :main_cols].astype(BF16)
        proj, gates = _proj(xt, w_main, w_gates_all[layer], tm_proj, main_cols // 4)

        gate_bias = jnp.pad(jnp.concatenate([ml_b_i[layer], ml_b_f[layer]]).astype(F32),
                            (0, LANES - 2 * ML_HEADS))[None, :]
        hg, ml, casts = _mix(proj, gates, lower_bounds[layer][None, :], hg_norm_g[layer][None, :].astype(F32),
                             ml_conv_w[layer].astype(F32), ml_conv_b[layer][None, :].astype(F32), gate_bias,
                             ml_norm_g[layer][None, :].astype(F32), B, S, rows, 4 * hg_w,
                             cast_srcs=[(moe_f32[name], jj) for name, jj in riders])
        for key, out in zip(riders, casts):
            if out is not None:
                moe_bf16[key] = out

        w_router = None
        if moe:
            wr = jnp.pad(moe_w_router[j].astype(F32), ((0, 0), (0, LANES - N_EXPERTS)))
            wr_hi = wr.astype(BF16)
            w_router = jnp.concatenate([wr_hi, (wr - wr_hi.astype(F32)).astype(BF16)], axis=1)
        res = _outproj(hg, ml, _cast_bf16(w_out, layer, D, D, wt, wt), xt,
                       ln_mix_g[layer][None, :], ln_mix_b[layer][None, :], w_router, tm)
        g2 = ln_ffn_g[layer][None, :]
        b2 = ln_ffn_b[layer][None, :]
        if not moe:
            (x1,) = res
            wg = _cast_bf16(ffn_w_gate, j, D, f_dense_p, wt, tf_dense)
            wu = _cast_bf16(ffn_w_up, j, D, f_dense_p, wt, tf_dense)
            wd = _cast_bf16(ffn_w_down, j, f_dense_p, D, tf_dense, wt)
            xt = _ffn(x1, wg, wu, wd, g2, b2, tm, tf_dense)
        else:
            x1, idx, gate = res
            dest, slot_token, block_expert, n_valid = _dispatch(idx[:, :TOP_K], blk, n_blocks)
            ys = _moe_ffn(block_expert, n_valid, slot_token, x1, moe_weight("gate", j), moe_weight("up", j),
                          moe_weight("down", j), blk, 1024)
            xt = _combine(dest, ys, x1, gate, g2, b2, _tile(T, 256))
    return xt.reshape(B, S, D).astype(x.dtype)
```

```python
import functools

import jax
import jax.numpy as jnp
import numpy as np
from jax import lax
from jax.experimental import pallas as pl
from jax.experimental.pallas import tpu as pltpu

F32 = jnp.float32
BF16 = jnp.bfloat16
HIGHEST = lax.Precision.HIGHEST

MODEL_DEPTH = 4
HG_HEADS = 8
HG_D = 128
ML_HEADS = 4
ML_DQK = 128
ML_DV = 256
CONV_W = 4
N_EXPERTS = 8
TOP_K = 2
ALPHA = (2 * MODEL_DEPTH) ** 0.25
LN_EPS = 1e-5
RMS_EPS = 1e-6
NEG_BIG = -1e30
F_MIN = 1e-30

LANES = 128
SUBLANES = 8
VMEM_LIMIT_BYTES = 56 * 1024 * 1024
SIDE_CAST_BLOCK_BYTES = 4 * 1024 * 1024

CHUNK = 64
CONV_PAD = SUBLANES


def _cparams(*sem):
    return pltpu.CompilerParams(dimension_semantics=sem, vmem_limit_bytes=VMEM_LIMIT_BYTES)


def _sigmoid(x):
    return 1.0 / (1.0 + jnp.exp(-x))


def _layer_norm(y, g, b):
    mu = jnp.mean(y, axis=-1, keepdims=True)
    yc = y - mu
    var = jnp.mean(yc * yc, axis=-1, keepdims=True)
    return yc * lax.rsqrt(var + LN_EPS) * g + b


def _cast_kernel(x_ref, o_ref, *, rows_in, cols_in, mask):
    x = x_ref[...]
    if mask:
        tr, tc = x.shape
        r = pl.program_id(0) * tr + lax.broadcasted_iota(jnp.int32, x.shape, 0)
        c = pl.program_id(1) * tc + lax.broadcasted_iota(jnp.int32, x.shape, 1)
        x = jnp.where((r < rows_in) & (c < cols_in), x, 0.0)
    o_ref[...] = x.astype(BF16)


def _cast_bf16(w, layer, rows_out, cols_out, tr, tc):
    _, R, C = w.shape
    mask = rows_out > R or cols_out > C
    return pl.pallas_call(
        functools.partial(_cast_kernel, rows_in=R, cols_in=C, mask=mask),
        grid=(rows_out // tr, cols_out // tc),
        in_specs=[pl.BlockSpec((None, tr, tc), lambda i, j: (layer, i, j))],
        out_specs=pl.BlockSpec((tr, tc), lambda i, j: (i, j)),
        out_shape=jax.ShapeDtypeStruct((rows_out, cols_out), BF16),
        compiler_params=_cparams("parallel", "parallel"),
        name="cast_bf16",
    )(w)


def _proj_kernel(x_ref, w_ref, wgate_ref, o_ref, ogate_ref):
    xb = x_ref[...].astype(BF16)
    o_ref[...] = jnp.dot(xb, w_ref[...], preferred_element_type=F32)

    @pl.when(pl.program_id(1) == 0)
    def _():
        ogate_ref[...] = jnp.dot(xb, wgate_ref[...], preferred_element_type=F32)


def _proj(x, w, w_gates, tm, tn):
    T, D = x.shape
    N = w.shape[1]
    return pl.pallas_call(
        _proj_kernel,
        grid=(T // tm, N // tn),
        in_specs=[pl.BlockSpec((tm, D), lambda i, j: (i, 0)),
                  pl.BlockSpec((D, tn), lambda i, j: (0, j)),
                  pl.BlockSpec((D, LANES), lambda i, j: (0, 0))],
        out_specs=[pl.BlockSpec((tm, tn), lambda i, j: (i, j)),
                   pl.BlockSpec((tm, LANES), lambda i, j: (i, 0))],
        out_shape=[jax.ShapeDtypeStruct((T, N), F32), jax.ShapeDtypeStruct((T, LANES), F32)],
        compiler_params=_cparams("parallel", "arbitrary"),
        name="in_proj",
    )(x, w, w_gates)


_HG_LEVELS = (32, 16, 8, 4, 2, 1)
_HG_DIAG = len(_HG_LEVELS)


def _hgrn2_tables():
    t = np.arange(CHUNK)[:, None]
    s = np.arange(CHUNK)[None, :]
    level = np.full((CHUNK, CHUNK), -1, np.int32)
    level[t == s] = _HG_DIAG
    tril = (s <= t).astype(np.float32)
    mats = [tril]
    for li, m in enumerate(_HG_LEVELS):
        mask = (t // (2 * m) == s // (2 * m)) & (t % (2 * m) >= m) & (s % (2 * m) < m)
        assert not (level[mask] >= 0).any()
        level[mask] = li
        if m < SUBLANES:
            mid = (t // (2 * m)) * (2 * m) + m
            mats.append(tril - (s <= mid - 1).astype(np.float32))
    assert (level[s < t] >= 0).all() and (level[s > t] == -1).all()
    return np.concatenate(mats, axis=0), level


def _mix_kernel(hq_ref, hf_ref, hi_ref, hg_ref, lb_ref, hng_ref, w_ref, lv_ref,
                mq_ref, mk_ref, mv_ref, mog_ref, gt_ref, cw_ref, cb_ref, gb_ref, mng_ref, *rest,
                n_chunks, n_side):
    side_in = rest[:n_side]
    ho_ref, mo_ref = rest[n_side:n_side + 2]
    side_out = rest[n_side + 2:2 * n_side + 2]
    st_ref, cq_ref, ck_ref, c_ref, n_ref, m_ref = rest[2 * n_side + 2:]
    for src_ref, dst_ref in zip(side_in, side_out):
        dst_ref[...] = src_ref[...].astype(BF16)

    rows = n_chunks * CHUNK
    W = ML_HEADS * ML_DQK
    nt = (((1,), (1,)), ((), ()))

    @pl.when(pl.program_id(1) == 0)
    def _():
        st_ref[...] = jnp.zeros_like(st_ref)
        cq_ref[0:CONV_PAD, :] = jnp.zeros((CONV_PAD, W), F32)
        ck_ref[0:CONV_PAD, :] = jnp.zeros((CONV_PAD, W), F32)
        c_ref[...] = jnp.zeros_like(c_ref)
        n_ref[...] = jnp.zeros_like(n_ref)
        m_ref[...] = jnp.zeros_like(m_ref)

    cq_ref[CONV_PAD:CONV_PAD + rows, :] = mq_ref[...]
    ck_ref[CONV_PAD:CONV_PAD + rows, :] = mk_ref[...]

    w = w_ref[...]
    lv = lv_ref[...]
    ones_bf = jnp.ones((HG_D, CHUNK), BF16)
    units = [(hh, c) for hh in range(HG_HEADS) for c in range(n_chunks)]
    rows_of = {u: slice(u[1] * CHUNK, (u[1] + 1) * CHUNK) for u in units}
    lanes_of = {u: slice(u[0] * HG_D, (u[0] + 1) * HG_D) for u in units}
    q, k, v_bf, wb, p, qd, upd, dec, o = {}, {}, {}, {}, {}, {}, {}, {}, {}

    def h_phase1(u):
        lb = lb_ref[:, lanes_of[u]]
        one_m_lb = 1.0 - lb
        qp = hq_ref[rows_of[u], lanes_of[u]]
        fp = hf_ref[rows_of[u], lanes_of[u]]
        q[u] = qp * _sigmoid(qp)
        e_f = jnp.exp(-fp)
        sig_f = 1.0 / (1.0 + e_f)
        log_f = jnp.log2(jnp.maximum(lb + one_m_lb * sig_f, F_MIN))
        k[u] = one_m_lb * (e_f * sig_f)
        v_bf[u] = hi_ref[rows_of[u], lanes_of[u]].astype(BF16)
        lf_hi = log_f.astype(BF16)
        lf_lo = (log_f - lf_hi.astype(F32)).astype(BF16)
        wb[u] = (jnp.dot(w, lf_hi, preferred_element_type=F32)
                 + jnp.dot(w, lf_lo, preferred_element_type=F32))

    def h_phase2(u):
        b = wb[u][0:CHUNK, :]
        b_last = b[CHUNK - 1:CHUNK, :]
        qd[u] = (q[u] * jnp.exp2(b)).astype(BF16)
        kd = k[u] * jnp.exp2(b_last - b)
        upd[u] = jnp.dot(hi_ref[rows_of[u], lanes_of[u]].T.astype(BF16), kd.astype(BF16),
                         preferred_element_type=F32)
        dec[u] = jnp.exp2(b_last)
        pc = jnp.where(lv == _HG_DIAG,
                       jnp.dot((q[u] * k[u]).astype(BF16), ones_bf, preferred_element_type=F32), 0.0)
        small = 0
        for li, m in enumerate(_HG_LEVELS):
            if m >= SUBLANES:
                zero = jnp.zeros((m, HG_D), F32)
                qs, ks = [], []
                for lo in range(0, CHUNK, 2 * m):
                    mid = lo + m
                    ref = b[mid - 1:mid, :]
                    ks += [k[u][lo:mid, :] * jnp.exp2(ref - b[lo:mid, :]), zero]
                    qs += [zero, q[u][mid:mid + m, :] * jnp.exp2(b[mid:mid + m, :] - ref)]
                qt = jnp.concatenate(qs, axis=0)
                kt = jnp.concatenate(ks, axis=0)
            else:
                small += 1
                d = wb[u][small * CHUNK:(small + 1) * CHUNK, :]
                qt = q[u] * jnp.exp2(jnp.minimum(d, 0.0))
                kt = k[u] * jnp.exp2(jnp.minimum(-d, 0.0))
            s_l = lax.dot_general(qt.astype(BF16), kt.astype(BF16), nt, preferred_element_type=F32)
            pc = jnp.where(lv == li, s_l, pc)
        p[u] = pc.astype(BF16)

    def h_phase3(hh):
        st = st_ref[hh]
        for c in range(n_chunks):
            u = (hh, c)
            o[u] = lax.dot_general(qd[u], st.astype(BF16), nt, preferred_element_type=F32)
            st = st * dec[u] + upd[u]
        st_ref[hh] = st

    def h_phase4(u):
        oc = o[u] + jnp.dot(p[u], v_bf[u], preferred_element_type=F32)
        gp = hg_ref[rows_of[u], lanes_of[u]]
        on = oc * lax.rsqrt(jnp.mean(oc * oc, axis=-1, keepdims=True) + RMS_EPS) * hng_ref[:, lanes_of[u]]
        ho_ref[rows_of[u], lanes_of[u]] = (on * (gp * _sigmoid(gp))).astype(ho_ref.dtype)

    row = lax.broadcasted_iota(jnp.int32, (CHUNK, CHUNK), 0)
    col = lax.broadcasted_iota(jnp.int32, (CHUNK, CHUNK), 1)
    causal = col <= row
    tril = causal.astype(F32)
    lane = lax.broadcasted_iota(jnp.int32, (CHUNK, LANES), 1)
    k_scale = ML_DQK ** -0.5
    xs, xts, mst = {}, {}, {}

    def m_gates(c):
        pre = gt_ref[c * CHUNK:(c + 1) * CHUNK, :] + gb_ref[...]
        log_f = jnp.minimum(pre, 0.0) - jnp.log1p(jnp.exp(-jnp.abs(pre)))
        bcum = jnp.dot(tril, log_f, precision=HIGHEST, preferred_element_type=F32)
        xs[c] = jnp.where(lane < ML_HEADS, pre, bcum)
        xts[c] = xs[c].T

    def m_first(c, h):
        r0 = c * CHUNK
        cs = slice(h * ML_DQK, (h + 1) * ML_DQK)
        vs = slice(h * ML_DV, (h + 1) * ML_DV)

        def conv(src_ref, w_off):
            acc = cb_ref[0:1, w_off + h * ML_DQK:w_off + (h + 1) * ML_DQK]
            for j in range(CONV_W):
                start = CONV_PAD + r0 - (CONV_W - 1) + j
                acc = acc + (cw_ref[j:j + 1, w_off + h * ML_DQK:w_off + (h + 1) * ML_DQK]
                             * src_ref[start:start + CHUNK, cs])
            return acc * _sigmoid(acc)

        qq = conv(cq_ref, 0)
        kk = conv(ck_ref, W) * k_scale
        v_b = mv_ref[r0:r0 + CHUNK, vs].astype(BF16)
        q_b = qq.astype(BF16)
        x, xt = xs[c], xts[c]
        li_c = x[:, h:h + 1]
        b_c = x[:, ML_HEADS + h:ML_HEADS + h + 1]
        li_r = xt[h:h + 1, :]
        b_r = xt[ML_HEADS + h:ML_HEADS + h + 1, :]
        log_intra = jnp.where(causal, b_c - b_r + li_r, NEG_BIG)
        mst[c, h] = dict(
            q=qq, k=kk, v_b=v_b, li_c=li_c, b_c=b_c, log_intra=log_intra,
            rmax=jnp.max(log_intra, axis=-1, keepdims=True),
            qk=lax.dot_general(q_b, kk.astype(BF16), nt, preferred_element_type=F32),
            qc=jnp.dot(q_b, c_ref[h].astype(BF16), preferred_element_type=F32),
            qn=jnp.sum(qq * n_ref[h][0:1, :], axis=-1, keepdims=True))

    def m_second(c, h):
        r0 = c * CHUNK
        vs = slice(h * ML_DV, (h + 1) * ML_DV)
        d = mst.pop((c, h))
        b_c, li_c = d["b_c"], d["li_c"]
        b_last = b_c[CHUNK - 1:CHUNK, :]
        m_prev = m_ref[h][0:1, 0:1]
        log_inter = b_c + m_prev
        m_t = jnp.maximum(log_inter, d["rmax"])
        w_inter = jnp.exp(log_inter - m_t)
        scores = d["qk"] * jnp.exp(d["log_intra"] - m_t)
        num = jnp.dot(scores.astype(BF16), d["v_b"], preferred_element_type=F32) + w_inter * d["qc"]
        den = jnp.sum(scores, axis=-1, keepdims=True) + w_inter * d["qn"]
        hh = num / jnp.maximum(jnp.abs(den), jnp.exp(-m_t))
        m_new = m_t[CHUNK - 1:CHUNK, :]
        kw = d["k"] * jnp.exp(b_last - b_c + li_c - m_new)
        w_carry = jnp.exp(b_last + m_prev - m_new)
        c_ref[h] = w_carry * c_ref[h] + jnp.dot(kw.T.astype(BF16), d["v_b"], preferred_element_type=F32)
        n_new = w_carry * n_ref[h][0:1, :] + jnp.sum(kw, axis=0, keepdims=True)
        n_ref[h] = jnp.broadcast_to(n_new, (SUBLANES, ML_DQK))
        m_ref[h] = jnp.broadcast_to(m_new, (SUBLANES, LANES))
        hn = hh * lax.rsqrt(jnp.mean(hh * hh, axis=-1, keepdims=True) + RMS_EPS) * mng_ref[0:1, vs]
        og = mog_ref[r0:r0 + CHUNK, vs]
        mo_ref[r0:r0 + CHUNK, vs] = (_sigmoid(og) * hn).astype(mo_ref.dtype)

    pairs = [(c, h) for c in range(n_chunks) for h in range(ML_HEADS)]
    for c in range(n_chunks):
        m_gates(c)
    h_pieces = ([functools.partial(h_phase1, u) for u in units]
                + [functools.partial(h_phase2, u) for u in units])
    m_pieces = []
    for i in range(len(pairs) + 1):
        if i < len(pairs):
            m_pieces.append(functools.partial(m_first, *pairs[i]))
        if i >= 1:
            m_pieces.append(functools.partial(m_second, *pairs[i - 1]))
    early, late = m_pieces[:len(m_pieces) // 2], m_pieces[len(m_pieces) // 2:]
    per = len(h_pieces) // len(early)
    for i, m_piece in enumerate(early):
        m_piece()
        for h_piece in h_pieces[i * per:(i + 1) * per]:
            h_piece()
    for h_piece in h_pieces[len(early) * per:]:
        h_piece()
    for hh in range(HG_HEADS):
        h_phase3(hh)
    per = len(units) // len(late)
    for i, m_piece in enumerate(late):
        m_piece()
        for u in units[i * per:(i + 1) * per]:
            h_phase4(u)
    for u in units[len(late) * per:]:
        h_phase4(u)
    cq_ref[0:CONV_PAD, :] = cq_ref[rows:rows + CONV_PAD, :]
    ck_ref[0:CONV_PAD, :] = ck_ref[rows:rows + CONV_PAD, :]


def _side_cast_rows(cast_src, n_steps):
    _, R, C = cast_src[0].shape
    rc = R // n_steps
    ok = R % n_steps == 0 and rc % 16 == 0 and C % LANES == 0 and rc * C * 4 <= SIDE_CAST_BLOCK_BYTES
    return rc if ok else 0


def _mix(proj, gates, lb, hg_ng, conv_w, conv_b, gate_bias, ml_ng, B, S, rows, col0, cast_srcs=()):
    T = B * S
    ns = S // rows
    n_chunks = rows // CHUNK
    wd = HG_HEADS * HG_D
    w_np, level_np = _hgrn2_tables()
    w = jnp.asarray(w_np, BF16)
    level = jnp.asarray(level_np, jnp.int32)
    W = ML_HEADS * ML_DQK
    V = ML_HEADS * ML_DV
    qb = col0 // W
    vb = (col0 + 2 * W) // V
    riding = [(i, src, _side_cast_rows(src, B * ns)) for i, src in enumerate(cast_srcs)]
    riding = [(i, src, rc) for i, src, rc in riding if rc]

    def rowmap(cb):
        return lambda b, j, cb=cb: (b * ns + j, cb)

    const = lambda b, j: (0, 0)
    in_specs = [pl.BlockSpec((rows, wd), rowmap(0)), pl.BlockSpec((rows, wd), rowmap(1)),
                pl.BlockSpec((rows, wd), rowmap(2)), pl.BlockSpec((rows, wd), rowmap(3)),
                pl.BlockSpec((1, wd), const), pl.BlockSpec((1, wd), const),
                pl.BlockSpec(w.shape, const), pl.BlockSpec(level.shape, const),
                pl.BlockSpec((rows, W), rowmap(qb)), pl.BlockSpec((rows, W), rowmap(qb + 1)),
                pl.BlockSpec((rows, V), rowmap(vb)), pl.BlockSpec((rows, V), rowmap(vb + 1)),
                pl.BlockSpec((rows, LANES), rowmap(0)),
                pl.BlockSpec((CONV_W, 2 * W), const), pl.BlockSpec((1, 2 * W), const),
                pl.BlockSpec((1, LANES), const), pl.BlockSpec((1, V), const)]
    out_specs = [pl.BlockSpec((rows, wd), rowmap(0)), pl.BlockSpec((rows, V), rowmap(0))]
    out_shape = [jax.ShapeDtypeStruct((T, wd), BF16), jax.ShapeDtypeStruct((T, V), BF16)]
    args = [proj, proj, proj, proj, lb, hg_ng, w, level, proj, proj, proj, proj, gates,
            conv_w, conv_b, gate_bias, ml_ng]
    for _, (w3d, w_layer), rc in riding:
        in_specs.append(pl.BlockSpec((None, rc, w3d.shape[2]), lambda b, j, wl=w_layer: (wl, b * ns + j, 0)))
        out_specs.append(pl.BlockSpec((rc, w3d.shape[2]), rowmap(0)))
        out_shape.append(jax.ShapeDtypeStruct(w3d.shape[1:], BF16))
        args.append(w3d)
    res = pl.pallas_call(
        functools.partial(_mix_kernel, n_chunks=n_chunks, n_side=len(riding)),
        grid=(B, ns),
        in_specs=in_specs, out_specs=out_specs, out_shape=out_shape,
        scratch_shapes=[pltpu.VMEM((HG_HEADS, HG_D, HG_D), F32),
                        pltpu.VMEM((rows + 2 * CONV_PAD, W), F32),
                        pltpu.VMEM((rows + 2 * CONV_PAD, W), F32),
                        pltpu.VMEM((ML_HEADS, ML_DQK, ML_DV), F32),
                        pltpu.VMEM((ML_HEADS, SUBLANES, ML_DQK), F32),
                        pltpu.VMEM((ML_HEADS, SUBLANES, LANES), F32)],
        compiler_params=_cparams("parallel", "arbitrary"),
        name="mixers",
    )(*args)
    casts = [None] * len(cast_srcs)
    for (i, _, _), out in zip(riding, res[2:]):
        casts[i] = out
    return res[0], res[1], casts


def _top2(logits):
    lane = lax.broadcasted_iota(jnp.int32, logits.shape, 1)
    lg = jnp.where(lane < N_EXPERTS, logits, -jnp.inf)
    m0 = jnp.max(lg, axis=-1, keepdims=True)
    i0 = jnp.min(jnp.where(lg == m0, lane, LANES), axis=-1, keepdims=True)
    lg1 = jnp.where(lane == i0, -jnp.inf, lg)
    m1 = jnp.max(lg1, axis=-1, keepdims=True)
    i1 = jnp.min(jnp.where(lg1 == m1, lane, LANES), axis=-1, keepdims=True)
    e1 = jnp.exp(m1 - m0)
    g0 = 1.0 / (1.0 + e1)
    g1 = e1 / (1.0 + e1)
    idx = jnp.where(lane == 0, i0, jnp.where(lane == 1, i1, 0))
    gate = jnp.where(lane == 0, g0, jnp.where(lane == 1, g1, 0.0))
    return idx, gate


def _outproj_kernel(hg_ref, ml_ref, w_ref, x_ref, g_ref, b_ref, *rest, route):
    if route:
        wr_ref, o_ref, idx_ref, gate_ref = rest
    else:
        (o_ref,) = rest
    half = hg_ref.shape[1]
    tm = x_ref.shape[0]
    parts = [slice(0, tm // 2), slice(tm // 2, tm)]
    mixed = [jnp.dot(hg_ref[r, :], w_ref[0:half, :], preferred_element_type=F32)
             + jnp.dot(ml_ref[r, :], w_ref[half:, :], preferred_element_type=F32) for r in parts]
    y = jnp.concatenate([_layer_norm(ALPHA * x_ref[r, :] + m, g_ref[...], b_ref[...])
                         for r, m in zip(parts, mixed)], axis=0)
    o_ref[...] = y
    if route:
        y_hi = y.astype(BF16)
        y_lo = (y - y_hi.astype(F32)).astype(BF16)
        r_hi = jnp.dot(y_hi, wr_ref[...], preferred_element_type=F32)
        r_lo = jnp.dot(y_lo, wr_ref[:, 0:LANES], preferred_element_type=F32)
        logits = r_hi[:, 0:LANES] + r_hi[:, LANES:] + r_lo
        idx, gate = _top2(logits)
        idx_ref[...] = idx
        gate_ref[...] = gate


def _outproj(hg, ml, w, x, g, b, w_router, tm):
    T, D = x.shape
    half = hg.shape[1]
    route = w_router is not None
    row = lambda i: (i, 0)
    const = lambda i: (0, 0)
    in_specs = [pl.BlockSpec((tm, half), row), pl.BlockSpec((tm, half), row),
                pl.BlockSpec((D, D), const), pl.BlockSpec((tm, D), row),
                pl.BlockSpec((1, D), const), pl.BlockSpec((1, D), const)]
    out_specs = [pl.BlockSpec((tm, D), row)]
    out_shape = [jax.ShapeDtypeStruct((T, D), F32)]
    args = [hg, ml, w, x, g, b]
    if route:
        in_specs.append(pl.BlockSpec((D, 2 * LANES), const))
        out_specs += [pl.BlockSpec((tm, LANES), row), pl.BlockSpec((tm, LANES), row)]
        out_shape += [jax.ShapeDtypeStruct((T, LANES), jnp.int32),
                      jax.ShapeDtypeStruct((T, LANES), F32)]
        args.append(w_router)
    return pl.pallas_call(
        functools.partial(_outproj_kernel, route=route),
        grid=(T // tm,),
        in_specs=in_specs, out_specs=out_specs, out_shape=out_shape,
        compiler_params=_cparams("parallel"),
        name="out_proj_route" if route else "out_proj",
    )(*args)


def _ffn_kernel(x_ref, wg_ref, wu_ref, wd_ref, g_ref, b_ref, o_ref, xb_ref):
    f = pl.program_id(1)

    @pl.when(f == 0)
    def _():
        xb_ref[...] = x_ref[...].astype(BF16)
        o_ref[...] = jnp.zeros_like(o_ref)

    xb = xb_ref[...]
    gt = jnp.dot(xb, wg_ref[...], preferred_element_type=F32)
    up = jnp.dot(xb, wu_ref[...], preferred_element_type=F32)
    h = (gt * _sigmoid(gt) * up).astype(BF16)
    o_ref[...] += jnp.dot(h, wd_ref[...], preferred_element_type=F32)

    @pl.when(f == pl.num_programs(1) - 1)
    def _():
        o_ref[...] = _layer_norm(ALPHA * x_ref[...] + o_ref[...], g_ref[...], b_ref[...])


def _ffn(x, wg, wu, wd, g, b, tm, tf):
    T, D = x.shape
    Fp = wg.shape[1]
    return pl.pallas_call(
        _ffn_kernel,
        grid=(T // tm, Fp // tf),
        in_specs=[pl.BlockSpec((tm, D), lambda i, f: (i, 0)),
                  pl.BlockSpec((D, tf), lambda i, f: (0, f)),
                  pl.BlockSpec((D, tf), lambda i, f: (0, f)),
                  pl.BlockSpec((tf, D), lambda i, f: (f, 0)),
                  pl.BlockSpec((1, D), lambda i, f: (0, 0)),
                  pl.BlockSpec((1, D), lambda i, f: (0, 0))],
        out_specs=pl.BlockSpec((tm, D), lambda i, f: (i, 0)),
        out_shape=jax.ShapeDtypeStruct((T, D), F32),
        scratch_shapes=[pltpu.VMEM((tm, D), BF16)],
        compiler_params=_cparams("parallel", "arbitrary"),
        name="dense_ffn",
    )(x, wg, wu, wd, g, b)


def _gather_rows(src_hbm, idx_ref, dst_ref, sem, n, *, wait, two_dma_threads=False):
    if wait:
        pltpu.make_async_copy(src_hbm.at[pl.ds(0, n), :], dst_ref, sem).wait()
        return
    for r in range(n):
        pltpu.make_async_copy(src_hbm.at[pl.ds(idx_ref[0, 0, r], 1), :],
                              dst_ref.at[pl.ds(r, 1), :], sem).start(priority=r % 2 if two_dma_threads else 0)


def _moe_kernel(be_ref, nv_ref, tok_ref, tok_next_ref, x_hbm, wg_ref, wu_ref, wd_ref, o_ref,
                xg_ref, xb_ref, sem, *, blk):
    del be_ref
    i = pl.program_id(0)
    f = pl.program_id(1)
    n_valid = nv_ref[0]
    live = i < n_valid
    slot = i % 2

    @pl.when(f == 0)
    def _():
        o_ref[...] = jnp.zeros_like(o_ref)

    @pl.when((i == 0) & (f == 0))
    def _():
        _gather_rows(x_hbm, tok_ref, xg_ref.at[0], sem.at[0], blk, wait=False)

    @pl.when(live & (f == 0))
    def _():
        _gather_rows(x_hbm, tok_ref, xg_ref.at[slot], sem.at[slot], blk, wait=True)
        xb_ref[...] = xg_ref[slot].astype(BF16)

    @pl.when((i + 1 < n_valid) & (f == 0))
    def _():
        _gather_rows(x_hbm, tok_next_ref, xg_ref.at[1 - slot], sem.at[1 - slot], blk, wait=False)

    @pl.when(live)
    def _():
        xb = xb_ref[...]
        gt = jnp.dot(xb, wg_ref[...], preferred_element_type=F32)
        up = jnp.dot(xb, wu_ref[...], preferred_element_type=F32)
        h = (gt * _sigmoid(gt) * up).astype(BF16)
        o_ref[...] += jnp.dot(h, wd_ref[...], preferred_element_type=F32)


def _moe_ffn(block_expert, n_valid, slot_token, x, wg, wu, wd, blk, tf):
    T, D = x.shape
    F = wg.shape[2]
    n_blocks = block_expert.shape[0]
    nf = F // tf
    tok3 = slot_token.reshape(n_blocks, 1, blk)

    def f_eff(i, f, nv):
        return jnp.where(i < nv[0], f, nf - 1)

    grid_spec = pltpu.PrefetchScalarGridSpec(
        num_scalar_prefetch=2,
        grid=(n_blocks, nf),
        in_specs=[pl.BlockSpec((1, 1, blk), lambda i, f, be, nv: (i, 0, 0), memory_space=pltpu.SMEM),
                  pl.BlockSpec((1, 1, blk), lambda i, f, be, nv: (jnp.minimum(i + 1, n_blocks - 1), 0, 0),
                               memory_space=pltpu.SMEM),
                  pl.BlockSpec(memory_space=pl.ANY),
                  pl.BlockSpec((None, D, tf), lambda i, f, be, nv: (be[i], 0, f_eff(i, f, nv))),
                  pl.BlockSpec((None, D, tf), lambda i, f, be, nv: (be[i], 0, f_eff(i, f, nv))),
                  pl.BlockSpec((None, tf, D), lambda i, f, be, nv: (be[i], f_eff(i, f, nv), 0))],
        out_specs=pl.BlockSpec((blk, D), lambda i, f, be, nv: (i, 0)),
        scratch_shapes=[pltpu.VMEM((2, blk, D), F32), pltpu.VMEM((blk, D), BF16),
                        pltpu.SemaphoreType.DMA((2,))],
    )
    return pl.pallas_call(
        functools.partial(_moe_kernel, blk=blk),
        grid_spec=grid_spec,
        out_shape=jax.ShapeDtypeStruct((n_blocks * blk, D), F32),
        compiler_params=_cparams("arbitrary", "arbitrary"),
        name="moe_ffn",
    )(block_expert, n_valid, tok3, tok3, x, wg, wu, wd)


def _combine_kernel(dest_ref, dest_next_ref, ys_hbm, x_ref, gate_ref, g_ref, b_ref, o_ref,
                    buf_ref, sem, *, tm):
    i = pl.program_id(0)
    slot = i % 2
    n = TOP_K * tm

    @pl.when(i == 0)
    def _():
        _gather_rows(ys_hbm, dest_ref, buf_ref.at[0], sem.at[0], n, wait=False, two_dma_threads=True)

    @pl.when(i + 1 < pl.num_programs(0))
    def _():
        _gather_rows(ys_hbm, dest_next_ref, buf_ref.at[1 - slot], sem.at[1 - slot], n, wait=False,
                     two_dma_threads=True)

    _gather_rows(ys_hbm, dest_ref, buf_ref.at[slot], sem.at[slot], n, wait=True)
    gate = gate_ref[...]
    ff = buf_ref[slot, 0:tm, :] * gate[:, 0:1] + buf_ref[slot, tm:2 * tm, :] * gate[:, 1:2]
    o_ref[...] = _layer_norm(ALPHA * x_ref[...] + ff, g_ref[...], b_ref[...])


def _combine(dest, ys, x, gate, g, b, tm):
    T, D = x.shape
    nt = T // tm
    dest3 = dest.reshape(nt, tm, TOP_K).transpose(0, 2, 1).reshape(nt, 1, TOP_K * tm)
    return pl.pallas_call(
        functools.partial(_combine_kernel, tm=tm),
        grid=(nt,),
        in_specs=[pl.BlockSpec((1, 1, TOP_K * tm), lambda i: (i, 0, 0), memory_space=pltpu.SMEM),
                  pl.BlockSpec((1, 1, TOP_K * tm), lambda i: (jnp.minimum(i + 1, nt - 1), 0, 0),
                               memory_space=pltpu.SMEM),
                  pl.BlockSpec(memory_space=pl.ANY),
                  pl.BlockSpec((tm, D), lambda i: (i, 0)),
                  pl.BlockSpec((tm, LANES), lambda i: (i, 0)),
                  pl.BlockSpec((1, D), lambda i: (0, 0)),
                  pl.BlockSpec((1, D), lambda i: (0, 0))],
        out_specs=pl.BlockSpec((tm, D), lambda i: (i, 0)),
        out_shape=jax.ShapeDtypeStruct((T, D), F32),
        scratch_shapes=[pltpu.VMEM((2, TOP_K * tm, D), F32), pltpu.SemaphoreType.DMA((2,))],
        compiler_params=_cparams("arbitrary"),
        name="moe_combine",
    )(dest3, dest3, ys, x, gate, g, b)


def _dispatch(idx2, blk, n_blocks):
    T = idx2.shape[0]
    n_assign = T * TOP_K
    expert = idx2.reshape(-1)
    onehot = (expert[:, None] == jnp.arange(N_EXPERTS, dtype=jnp.int32)[None, :]).astype(jnp.int32)
    csum = jnp.cumsum(onehot, axis=0)
    rank = jnp.sum(onehot * (csum - 1), axis=1)
    counts = csum[-1]
    padded = (counts + blk - 1) // blk * blk
    padded_end = jnp.cumsum(padded)
    dest = (padded_end - padded)[expert] + rank
    token = jnp.arange(n_assign, dtype=jnp.int32) // TOP_K
    slot_token = jnp.zeros((n_blocks * blk,), jnp.int32).at[dest].set(token)
    block_expert = jnp.minimum(
        jnp.searchsorted(padded_end, jnp.arange(n_blocks, dtype=jnp.int32) * blk, side="right"),
        N_EXPERTS - 1).astype(jnp.int32)
    n_valid = (padded_end[-1:] // blk).astype(jnp.int32)
    block_ids = jnp.arange(n_blocks, dtype=jnp.int32)
    block_expert = jnp.where(block_ids < n_valid[0], block_expert,
                             block_expert[jnp.maximum(n_valid[0] - 1, 0)])
    return dest.reshape(T, TOP_K).astype(jnp.int32), slot_token, block_expert, n_valid


def _tile(n, pref):
    t = min(n, pref)
    while n % t:
        t //= 2
    return t


def kernel(x, w_in, w_out, hg_lb_logits, hg_norm_g, ml_conv_w, ml_conv_b, ml_b_i, ml_b_f, ml_norm_g,
           ln_mix_g, ln_mix_b, ln_ffn_g, ln_ffn_b, ffn_w_gate, ffn_w_up, ffn_w_down,
           moe_w_router, moe_w_gate, moe_w_up, moe_w_down):
    B, S, D = x.shape
    T = B * S
    depth = w_in.shape[0]
    hg_w = HG_HEADS * HG_D
    main_cols = 4 * hg_w + 2 * ML_HEADS * ML_DQK + 2 * ML_HEADS * ML_DV

    p = jax.nn.softmax(hg_lb_logits.astype(F32), axis=0)
    lower_bounds = jnp.cumsum(p, axis=0) - p[0]

    rows = _tile(S, 256)
    tm_proj = _tile(T, 1024)
    tm = _tile(T, 512)
    blk = _tile(T, 512)
    n_blocks = -(-(T * TOP_K + N_EXPERTS * (blk - 1)) // blk)
    wt = 1024

    f_dense = ffn_w_gate.shape[2]
    f_moe = moe_w_gate.shape[3]
    tf_dense = 512
    f_dense_p = -(-f_dense // tf_dense) * tf_dense

    w_gates_all = jnp.pad(lax.slice_in_dim(w_in, main_cols, w_in.shape[2], axis=2),
                          ((0, 0), (0, 0), (0, LANES - 2 * ML_HEADS))).astype(BF16)

    n_layers_moe = moe_w_gate.shape[0]
    moe_f32 = {"gate": moe_w_gate.reshape(n_layers_moe, N_EXPERTS * D, f_moe),
               "up": moe_w_up.reshape(n_layers_moe, N_EXPERTS * D, f_moe),
               "down": moe_w_down.reshape(n_layers_moe, N_EXPERTS * f_moe, D)}
    moe_bf16 = {}

    def moe_weight(name, j):
        w = moe_bf16.get((name, j))
        if w is None:
            _, R, C = moe_f32[name].shape
            w = _cast_bf16(moe_f32[name], j, R, C, wt, wt)
        return w.reshape((N_EXPERTS, D, f_moe) if name != "down" else (N_EXPERTS, f_moe, D))

    xt = x.reshape(T, D).astype(F32)
    for layer in range(depth):
        moe = layer % 2 == 1
        j = layer // 2
        if moe:
            riders = [("down", j)]
        elif layer + 1 < depth:
            riders = [("gate", j), ("up", j)]
        else:
            riders = []

        w_main = w_in[layer, :, :main_cols].astype(BF16)
        proj, gates = _proj(xt, w_main, w_gates_all[layer], tm_proj, main_cols // 4)

        gate_bias = jnp.pad(jnp.concatenate([ml_b_i[layer], ml_b_f[layer]]).astype(F32),
                            (0, LANES - 2 * ML_HEADS))[None, :]
        hg, ml, casts = _mix(proj, gates, lower_bounds[layer][None, :], hg_norm_g[layer][None, :].astype(F32),
                             ml_conv_w[layer].astype(F32), ml_conv_b[layer][None, :].astype(F32), gate_bias,
                             ml_norm_g[layer][None, :].astype(F32), B, S, rows, 4 * hg_w,
                             cast_srcs=[(moe_f32[name], jj) for name, jj in riders])
        for key, out in zip(riders, casts):
            if out is not None:
                moe_bf16[key] = out

        w_router = None
        if moe:
            wr = jnp.pad(moe_w_router[j].astype(F32), ((0, 0), (0, LANES - N_EXPERTS)))
            wr_hi = wr.astype(BF16)
            w_router = jnp.concatenate([wr_hi, (wr - wr_hi.astype(F32)).astype(BF16)], axis=1)
        res = _outproj(hg, ml, _cast_bf16(w_out, layer, D, D, wt, wt), xt,
                       ln_mix_g[layer][None, :], ln_mix_b[layer][None, :], w_router, tm)
        g2 = ln_ffn_g[layer][None, :]
        b2 = ln_ffn_b[layer][None, :]
        if not moe:
            (x1,) = res
            wg = _cast_bf16(ffn_w_gate, j, D, f_dense_p, wt, tf_dense)
            wu = _cast_bf16(ffn_w_up, j, D, f_dense_p, wt, tf_dense)
            wd = _cast_bf16(ffn_w_down, j, f_dense_p, D, tf_dense, wt)
            xt = _ffn(x1, wg, wu, wd, g2, b2, tm, tf_dense)
        else:
            x1, idx, gate = res
            dest, slot_token, block_expert, n_valid = _dispatch(idx[:, :TOP_K], blk, n_blocks)
            ys = _moe_ffn(block_expert, n_valid, slot_token, x1, moe_weight("gate", j), moe_weight("up", j),
                          moe_weight("down", j), blk, 1024)
            xt = _combine(dest, ys, x1, gate, g2, b2, _tile(T, 256))
    return xt.reshape(B, S, D).astype(x.dtype)
```
